```python
import math
import jax
import jax.numpy as jnp
from jax import lax
import numpy as np

D_MODEL = 1024
BATCH = 32
SEQ = 2048
DEPTH = 4

MIX_WIDTH = D_MODEL
ATTN_HEAD_DIM = 64
ATTN_WIDTH = MIX_WIDTH // 4
ATTN_HEADS = ATTN_WIDTH // ATTN_HEAD_DIM
DILATED_PATTERNS = ((128, 1), (512, 4), (2048, 16))
ROPE_THETA = 10000.0
CONV_CH = MIX_WIDTH // 4
CONV_WIDTH = 31
GDN_HEAD_DIM = 128
GDN_WIDTH = MIX_WIDTH - ATTN_WIDTH - CONV_CH
GDN_HEADS = GDN_WIDTH // GDN_HEAD_DIM
GDN_CONV_WIDTH = 4
GDN_CHUNK = 64
FFN_HIDDEN = ((8 * D_MODEL + 3 * 256 - 1) // (3 * 256)) * 256
IN_SPLIT_SIZES = (ATTN_WIDTH, ATTN_WIDTH, ATTN_WIDTH, 2 * CONV_CH, 3 * GDN_WIDTH, GDN_WIDTH, GDN_HEADS, GDN_HEADS)
IN_WIDTH = sum(IN_SPLIT_SIZES)
RMS_EPS = 1e-6
LN_EPS = 1e-5
L2_EPS = 1e-6

kernel_name = "hybrid_dilated_conformer_gdn_trunk"


def rms_norm(x, gain):
    x32 = x.astype(jnp.float32)
    y = x32 * lax.rsqrt(jnp.mean(x32 * x32, axis=-1, keepdims=True) + RMS_EPS)
    return (y * gain.astype(jnp.float32)).astype(x.dtype)


def l2_normalize(t):
    return t * lax.rsqrt(jnp.sum(t * t, axis=-1, keepdims=True) + L2_EPS)


def causal_depthwise_conv(x, w):
    K, C = w.shape
    return lax.conv_general_dilated(
        x, w.reshape(K, 1, C).astype(x.dtype), window_strides=(1,), padding=[(K - 1, 0)],
        dimension_numbers=('NWC', 'WIO', 'NWC'), feature_group_count=C)


def rotary(t, positions):
    D = t.shape[-1]
    half = D // 2
    inv_freq = jnp.exp(-math.log(ROPE_THETA) * jnp.arange(half, dtype=jnp.float32) * (2.0 / D))
    ang = positions.astype(jnp.float32)[:, None] * inv_freq[None, :]
    cos = jnp.cos(ang)[None, :, None, :]
    sin = jnp.sin(ang)[None, :, None, :]
    t1, t2 = t[..., :half], t[..., half:]
    return jnp.concatenate([t1 * cos - t2 * sin, t2 * cos + t1 * sin], axis=-1)


def dilated_window_attention(q, k, v, window, dilation):
    Bn, S, H, D = q.shape
    n = window // dilation
    L = S // dilation
    nb = -(-L // n)
    Lp = nb * n

    def residues(t):
        t = t.reshape(Bn, L, dilation, H, D)
        return jnp.pad(t, ((0, 0), (0, Lp - L), (0, 0), (0, 0), (0, 0)))

    def key_blocks(t):
        tp = jnp.pad(residues(t), ((0, 0), (n, 0), (0, 0), (0, 0), (0, 0)))
        tp = tp.reshape(Bn, nb + 1, n, dilation, H, D)
        return jnp.concatenate([tp[:, :-1], tp[:, 1:]], axis=2)

    qb = residues(q).reshape(Bn, nb, n, dilation, H, D)
    kb = key_blocks(k)
    vb = key_blocks(v)
    s = jnp.einsum('bnqrhd,bnkrhd->bnrhqk', qb, kb)
    qi = jnp.arange(n)[None, :, None]
    ki = jnp.arange(2 * n)[None, None, :]
    blk = jnp.arange(nb)[:, None, None]
    valid = (ki >= qi) & (ki <= qi + n) & (blk * n - n + ki >= 0)
    s = jnp.where(valid[None, :, None, None], s, -jnp.inf)
    m = jnp.max(s, axis=-1, keepdims=True)
    p = jnp.exp(s - m)
    l = jnp.sum(p, axis=-1, keepdims=True)
    o = jnp.einsum('bnrhqk,bnkrhd->bnqrhd', p / l, vb)
    o = o.reshape(Bn, Lp, dilation, H, D)[:, :L].reshape(Bn, S, H, D)
    lse = (m + jnp.log(l))[..., 0]
    lse = lse.transpose(0, 1, 4, 2, 3).reshape(Bn, Lp, dilation, H)[:, :L].reshape(Bn, S, H)
    return o, lse


def dilated_attention_mixer(aq, ak, av, positions):
    Bn, S, _ = aq.shape
    shp = (Bn, S, ATTN_HEADS, ATTN_HEAD_DIM)
    q = rotary(aq.reshape(shp).astype(jnp.float32), positions) * (ATTN_HEAD_DIM ** -0.5)
    k = rotary(ak.reshape(shp).astype(jnp.float32), positions)
    v = av.reshape(shp).astype(jnp.float32)
    outs, lses = [], []
    for window, dilation in DILATED_PATTERNS:
        o, lse = dilated_window_attention(q, k, v, window, dilation)
        outs.append(o)
        lses.append(lse)
    wts = jax.nn.softmax(jnp.stack(lses, axis=0), axis=0)
    o = jnp.einsum('pbsh,pbshd->bshd', wts, jnp.stack(outs, axis=0))
    return o.reshape(Bn, S, ATTN_WIDTH)


def conformer_conv_mixer(u, dw, dw_bias, norm_gain, norm_bias):
    val, gate = jnp.split(u, 2, axis=-1)
    y = val * jax.nn.sigmoid(gate)
    y = causal_depthwise_conv(y, dw) + dw_bias.astype(y.dtype)
    y32 = y.astype(jnp.float32)
    mu = jnp.mean(y32, axis=-1, keepdims=True)
    var = jnp.mean(jnp.square(y32 - mu), axis=-1, keepdims=True)
    y32 = (y32 - mu) * lax.rsqrt(var + LN_EPS) * norm_gain.astype(jnp.float32) + norm_bias.astype(jnp.float32)
    return jax.nn.silu(y32)


def chunk_gated_delta_rule(q, k, v, g, beta):
    Bn, S, H, Dk = q.shape
    Dv = v.shape[-1]
    C = GDN_CHUNK
    N = S // C
    q = q * (Dk ** -0.5)

    def to_chunks(t):
        return t.reshape(Bn, N, C, H, -1).transpose(0, 3, 1, 2, 4)

    q, k, v = to_chunks(q), to_chunks(k), to_chunks(v)
    g = jnp.cumsum(g.reshape(Bn, N, C, H).transpose(0, 3, 1, 2), axis=-1)
    beta = beta.reshape(Bn, N, C, H).transpose(0, 3, 1, 2)
    idx = jnp.arange(C)
    lower_incl = idx[:, None] >= idx[None, :]
    strict = idx[:, None] > idx[None, :]
    decay = jnp.exp(jnp.where(lower_incl, g[..., :, None] - g[..., None, :], -jnp.inf))
    k_beta = k * beta[..., None]
    v_beta = v * beta[..., None]
    m = jnp.where(strict, jnp.einsum('bhncd,bhnsd->bhncs', k_beta, k) * decay, 0.0)
    rhs = jnp.concatenate([v_beta, k_beta * jnp.exp(g)[..., None]], axis=-1)
    sol = lax.linalg.triangular_solve(m + jnp.eye(C, dtype=jnp.float32), rhs, left_side=True, lower=True,
                                      unit_diagonal=True)
    u = sol[..., :Dv]
    w = sol[..., Dv:]
    intra = jnp.where(lower_incl, jnp.einsum('bhncd,bhnsd->bhncs', q, k) * decay, 0.0)

    def step(state, xs):
        q_c, k_c, u_c, w_c, g_c, intra_c = xs
        v_new = u_c - w_c @ state
        out = (q_c * jnp.exp(g_c)[..., None]) @ state + intra_c @ v_new
        g_last = g_c[..., -1]
        state = state * jnp.exp(g_last)[..., None, None] + jnp.einsum(
            'bhcd,bhce->bhde', k_c * jnp.exp(g_last[..., None] - g_c)[..., None], v_new)
        return state, out

    xs = tuple(jnp.moveaxis(t, 2, 0) for t in (q, k, u, w, g, intra))
    state0 = jnp.zeros((Bn, H, Dk, Dv), jnp.float32)
    _, out = lax.scan(step, state0, xs)
    return out.transpose(1, 0, 3, 2, 4).reshape(Bn, S, H, Dv)


def gated_deltanet_mixer(qkv, gate, a, b, short_conv, a_log, dt_bias, out_norm):
    Bn, S, _ = qkv.shape
    qkv = jax.nn.silu(causal_depthwise_conv(qkv, short_conv).astype(jnp.float32))
    shp = (Bn, S, GDN_HEADS, GDN_HEAD_DIM)
    q, k, v = (t.reshape(shp) for t in jnp.split(qkv, 3, axis=-1))
    q = l2_normalize(q)
    k = l2_normalize(k)
    beta = jax.nn.sigmoid(b.astype(jnp.float32))
    g = -jnp.exp(a_log.astype(jnp.float32)) * jax.nn.softplus(a.astype(jnp.float32) + dt_bias.astype(jnp.float32))
    o = chunk_gated_delta_rule(q, k, v, g, beta)
    o = rms_norm(o, out_norm) * jax.nn.silu(gate.astype(jnp.float32).reshape(shp))
    return o.reshape(Bn, S, GDN_WIDTH)


def setup_inputs(seed: int = 0) -> dict:
    key = jax.random.key(seed)
    ks = jax.random.split(key, 20)
    f32 = jnp.float32

    def nrm(k, shape, fan_in):
        return jax.random.normal(k, shape, f32) * (fan_in ** -0.5)

    def gain(k, shape):
        return 1.0 + 0.01 * jax.random.normal(k, shape, f32)

    dt = jnp.exp(jax.random.uniform(ks[10], (DEPTH, GDN_HEADS), f32, math.log(1e-3), math.log(1e-1)))
    return {
        'x': jax.random.normal(ks[0], (BATCH, SEQ, D_MODEL), f32),
        'attn_pre_norm': gain(ks[1], (DEPTH, D_MODEL)),
        'w_in': nrm(ks[2], (DEPTH, D_MODEL, IN_WIDTH), D_MODEL),
        'conv_dw': nrm(ks[3], (DEPTH, CONV_WIDTH, CONV_CH), CONV_WIDTH),
        'conv_dw_bias': 0.01 * jax.random.normal(ks[4], (DEPTH, CONV_CH), f32),
        'conv_norm_gain': gain(ks[5], (DEPTH, CONV_CH)),
        'conv_norm_bias': 0.01 * jax.random.normal(ks[6], (DEPTH, CONV_CH), f32),
        'gdn_short_conv': nrm(ks[7], (DEPTH, GDN_CONV_WIDTH, 3 * GDN_WIDTH), GDN_CONV_WIDTH),
        'gdn_a_log': jnp.log(jax.random.uniform(ks[8], (DEPTH, GDN_HEADS), f32, 1.0, 16.0)),
        'gdn_dt_bias': dt + jnp.log(-jnp.expm1(-dt)),
        'gdn_out_norm': gain(ks[9], (DEPTH, GDN_HEAD_DIM)),
        'w_out': nrm(ks[11], (DEPTH, MIX_WIDTH, D_MODEL), MIX_WIDTH),
        'attn_post_norm': gain(ks[12], (DEPTH, D_MODEL)),
        'ffn_pre_norm': gain(ks[13], (DEPTH, D_MODEL)),
        'w_gate': nrm(ks[14], (DEPTH, D_MODEL, FFN_HIDDEN), D_MODEL),
        'w_up': nrm(ks[15], (DEPTH, D_MODEL, FFN_HIDDEN), D_MODEL),
        'w_down': nrm(ks[16], (DEPTH, FFN_HIDDEN, D_MODEL), FFN_HIDDEN),
        'ffn_post_norm': gain(ks[17], (DEPTH, D_MODEL)),
    }


def reference(x, attn_pre_norm, w_in, conv_dw, conv_dw_bias, conv_norm_gain, conv_norm_bias, gdn_short_conv,
              gdn_a_log, gdn_dt_bias, gdn_out_norm, w_out, attn_post_norm, ffn_pre_norm, w_gate, w_up, w_down,
              ffn_post_norm):
    Bn, S, _ = x.shape
    positions = jnp.arange(S)
    offsets = np.cumsum(IN_SPLIT_SIZES)[:-1].tolist()
    for i in range(DEPTH):
        h = rms_norm(x, attn_pre_norm[i])
        z = h @ w_in[i]
        aq, ak, av, conv_u, gdn_qkv, gdn_gate, gdn_a, gdn_b = jnp.split(z, offsets, axis=-1)
        y_attn = dilated_attention_mixer(aq, ak, av, positions)
        y_conv = conformer_conv_mixer(conv_u, conv_dw[i], conv_dw_bias[i], conv_norm_gain[i], conv_norm_bias[i])
        y_gdn = gated_deltanet_mixer(gdn_qkv, gdn_gate, gdn_a, gdn_b, gdn_short_conv[i], gdn_a_log[i],
                                     gdn_dt_bias[i], gdn_out_norm[i])
        mix = jnp.concatenate([y_attn.astype(x.dtype), y_conv.astype(x.dtype), y_gdn.astype(x.dtype)], axis=-1)
        x = x + rms_norm(mix @ w_out[i], attn_post_norm[i])
        h = rms_norm(x, ffn_pre_norm[i])
        f = (jax.nn.silu(h @ w_gate[i]) * (h @ w_up[i])) @ w_down[i]
        x = x + rms_norm(f, ffn_post_norm[i])
    return x
```

```python
import functools
import math

import numpy as np
import jax
import jax.numpy as jnp
from jax import lax
from jax.experimental import pallas as pl
from jax.experimental.pallas import tpu as pltpu

F32 = jnp.float32
BF16 = jnp.bfloat16

ATTN_HEAD_DIM = 64
ATTN_HEADS = 4
ATTN_WIDTH = ATTN_HEADS * ATTN_HEAD_DIM
ATTN_BLOCK = 128
DILATIONS = (1, 4, 16)
ROPE_THETA = 10000.0
CONV_CH = 256
CONV_WIDTH = 31
GDN_HEADS = 4
GDN_HEAD_DIM = 128
GDN_WIDTH = GDN_HEADS * GDN_HEAD_DIM
GDN_CONV_WIDTH = 4
GDN_CHUNK = 64
RMS_EPS = 1e-6
LN_EPS = 1e-5
L2_EPS = 1e-6
NEG_BIG = -1e30

V7X_LANES = 128
V7X_SUBLANES = 8
V7X_VMEM_LIMIT_BYTES = 56 * 1024 * 1024

W_ATTN = 0
W_CONV = W_ATTN + 3 * ATTN_WIDTH
W_GQKV = W_CONV + 2 * CONV_CH
W_GGATE = W_GQKV + 3 * GDN_WIDTH
W_AB = W_GGATE + GDN_WIDTH
W_TOTAL = W_AB + V7X_LANES


def _rms(x):
    return x * lax.rsqrt(jnp.mean(x * x, axis=-1, keepdims=True) + RMS_EPS)


def _sigmoid(x):
    return 1.0 / (1.0 + jnp.exp(-x))


def _silu(x):
    return x * _sigmoid(x)


def _dot(a, b, precision=None):
    return jnp.dot(a, b, preferred_element_type=F32, precision=precision)


def _dot_nt(a, b, precision=None):
    return lax.dot_general(a, b, (((1,), (1,)), ((), ())), preferred_element_type=F32, precision=precision)


def _inproj_kernel(x_ref, gain_ref, cos_ref, sin_ref, w_ref, q1_ref, q2_ref, k1_ref, k2_ref, va_ref, vb_ref,
                   conv_ref, gqkv_ref, ggate_ref, ab_ref):
    h = (_rms(x_ref[...]) * gain_ref[...]).astype(BF16)

    def proj(lo, hi):
        return _dot(h, w_ref[:, lo:hi])

    qk = proj(W_ATTN, W_ATTN + 2 * ATTN_WIDTH)
    cos = cos_ref[...]
    sin = sin_ref[...]
    half = ATTN_WIDTH // 2
    q1, q2 = qk[:, 0:half], qk[:, half:2 * half]
    k1, k2 = qk[:, 2 * half:3 * half], qk[:, 3 * half:4 * half]
    scale = ATTN_HEAD_DIM ** -0.5
    q1_ref[...] = (q1 * cos - q2 * sin) * scale
    q2_ref[...] = (q2 * cos + q1 * sin) * scale
    k1_ref[...] = k1 * cos - k2 * sin
    k2_ref[...] = k2 * cos + k1 * sin
    v = proj(W_ATTN + 2 * ATTN_WIDTH, W_CONV)
    va_ref[...] = v[:, 0:half]
    vb_ref[...] = v[:, half:2 * half]
    conv_ref[...] = proj(W_CONV, W_GQKV)
    gqkv_ref[...] = proj(W_GQKV, W_GGATE)
    ggate_ref[...] = proj(W_GGATE, W_AB)
    ab_ref[...] = proj(W_AB, W_TOTAL)


def _inproj(x2d, gain, cos, sin, w, seq, tm):
    T, D = x2d.shape
    blocks_per_seq = seq // tm
    row = lambda i: (i, 0)
    const = lambda i: (0, 0)
    widths = (ATTN_WIDTH // 2,) * 6 + (2 * CONV_CH, 3 * GDN_WIDTH, GDN_WIDTH, V7X_LANES)
    return pl.pallas_call(
        _inproj_kernel,
        name="inproj",
        grid=(T // tm,),
        in_specs=[
            pl.BlockSpec((tm, D), row),
            pl.BlockSpec((1, D), const),
            pl.BlockSpec((tm, ATTN_WIDTH // 2), lambda i: (i % blocks_per_seq, 0)),
            pl.BlockSpec((tm, ATTN_WIDTH // 2), lambda i: (i % blocks_per_seq, 0)),
            pl.BlockSpec((D, W_TOTAL), const),
        ],
        out_specs=[pl.BlockSpec((tm, n), row) for n in widths],
        out_shape=[jax.ShapeDtypeStruct((T, n), F32) for n in widths],
        compiler_params=pltpu.CompilerParams(
            dimension_semantics=("arbitrary",), vmem_limit_bytes=V7X_VMEM_LIMIT_BYTES),
    )(x2d, gain, cos, sin, w)


def _attn_unit(q, k, v, valid):
    nq = q.shape[0]
    lane = lax.broadcasted_iota(jnp.int32, (1, ATTN_WIDTH), 1)
    qk_head = (lane % (ATTN_WIDTH // 2)) // (ATTN_HEAD_DIM // 2)
    v_head = lane // ATTN_HEAD_DIM
    qb = q.astype(BF16)
    zero = jnp.zeros_like(qb)
    q_stack = jnp.concatenate([jnp.where(qk_head == h, qb, zero) for h in range(ATTN_HEADS)], axis=0)
    s = _dot_nt(q_stack, k.astype(BF16))
    probs, inv_l, lse = [], [], []
    for h in range(ATTN_HEADS):
        s_h = jnp.where(valid, s[h * nq:(h + 1) * nq], NEG_BIG)
        m = jnp.max(s_h, axis=-1, keepdims=True)
        p = jnp.exp(s_h - m)
        l = jnp.sum(p, axis=-1, keepdims=True)
        probs.append(p.astype(BF16))
        inv_l.append(1.0 / l)
        lse.append(m + jnp.log(l))
    o = _dot(jnp.concatenate(probs, axis=0), v.astype(BF16))
    out = jnp.zeros((nq, ATTN_WIDTH), F32)
    lse_wide = jnp.zeros((nq, ATTN_WIDTH), F32)
    for h in range(ATTN_HEADS):
        sel = v_head == h
        out = jnp.where(sel, o[h * nq:(h + 1) * nq] * inv_l[h], out)
        lse_wide = jnp.where(sel, lse[h], lse_wide)
    return out, lse_wide


def _attn_kernel(q1_ref, q2_ref, k1_ref, k2_ref, va_ref, vb_ref, o_ref,
                 o2a_ref, o2b_ref, l2a_ref, l2b_ref, o3a_ref, o3b_ref, l3a_ref, l3b_ref):
    n = ATTN_BLOCK
    seq = q1_ref.shape[0]
    qi = lax.broadcasted_iota(jnp.int32, (n, 2 * n), 0)
    ki = lax.broadcasted_iota(jnp.int32, (n, 2 * n), 1)
    valid_band = (ki >= qi) & (ki <= qi + n)
    valid_first = lax.broadcasted_iota(jnp.int32, (n, n), 1) <= lax.broadcasted_iota(jnp.int32, (n, n), 0)

    def rows(start, size, stride):
        if stride == 1:
            return pl.ds(start, size)
        return pl.ds(start, size, stride=stride)

    def load(refs, sl):
        return jnp.concatenate([r[sl, :] for r in refs], axis=1)

    def store(refs, sl, val):
        for j, r in enumerate(refs):
            r[sl, :] = val[:, j * V7X_LANES:(j + 1) * V7X_LANES]

    def unit(q_start, stride, first):
        q = load((q1_ref, q2_ref), rows(q_start, n, stride))
        if first:
            k_rows = rows(q_start, n, stride)
            valid = valid_first
        else:
            k_rows = rows(q_start - n * stride, 2 * n, stride)
            valid = valid_band
        k = load((k1_ref, k2_ref), k_rows)
        v = load((va_ref, vb_ref), k_rows)
        return _attn_unit(q, k, v, valid)

    dilated = ((DILATIONS[1], (o2a_ref, o2b_ref), (l2a_ref, l2b_ref)),
               (DILATIONS[2], (o3a_ref, o3b_ref), (l3a_ref, l3b_ref)))
    for dil, o_refs, l_refs in dilated:
        span = n * dil
        nblk = seq // span

        def residue_body(r, carry, dil=dil, span=span, nblk=nblk, o_refs=o_refs, l_refs=l_refs):
            o, lse = unit(r, dil, True)
            store(o_refs, rows(r, n, dil), o)
            store(l_refs, rows(r, n, dil), lse)

            def blk_body(b, c):
                start = b * span + r
                o, lse = unit(start, dil, False)
                store(o_refs, rows(start, n, dil), o)
                store(l_refs, rows(start, n, dil), lse)
                return c

            if nblk > 1:
                lax.fori_loop(1, nblk, blk_body, 0)
            return carry

        lax.fori_loop(0, dil, residue_body, 0)

    def combine(start, o1, l1):
        sl = pl.ds(start, n)
        o2, l2 = load((o2a_ref, o2b_ref), sl), load((l2a_ref, l2b_ref), sl)
        o3, l3 = load((o3a_ref, o3b_ref), sl), load((l3a_ref, l3b_ref), sl)
        top = jnp.maximum(jnp.maximum(l1, l2), l3)
        w1, w2, w3 = jnp.exp(l1 - top), jnp.exp(l2 - top), jnp.exp(l3 - top)
        o_ref[sl, :] = (w1 * o1 + w2 * o2 + w3 * o3) / (w1 + w2 + w3)

    o1, l1 = unit(0, 1, True)
    combine(0, o1, l1)

    def dense_body(b, c):
        start = pl.multiple_of(b * n, n)
        o1, l1 = unit(start, 1, False)
        combine(start, o1, l1)
        return c

    lax.fori_loop(1, seq // n, dense_body, 0)


def _attention(parts):
    B, S, W = parts[0].shape
    blk = lambda b: (b, 0, 0)
    return pl.pallas_call(
        _attn_kernel,
        name="dilated_attn",
        grid=(B,),
        in_specs=[pl.BlockSpec((None, S, W), blk) for _ in parts],
        out_specs=pl.BlockSpec((None, S, ATTN_WIDTH), blk),
        out_shape=jax.ShapeDtypeStruct((B, S, ATTN_WIDTH), F32),
        scratch_shapes=[pltpu.VMEM((S, V7X_LANES), F32) for _ in range(8)],
        compiler_params=pltpu.CompilerParams(
            dimension_semantics=("arbitrary",), vmem_limit_bytes=V7X_VMEM_LIMIT_BYTES),
    )(*parts)


CONV_PAD = 32
CONV_ROWS = 64


def _conv_kernel(u_ref, dw_ref, bias_ref, gain_ref, nbias_ref, o_ref, g_ref):
    seq = u_ref.shape[0]
    g_ref[0:CONV_PAD, :] = jnp.zeros((CONV_PAD, CONV_CH), F32)
    g_ref[CONV_PAD:CONV_PAD + seq, :] = u_ref[:, 0:CONV_CH] * _sigmoid(u_ref[:, CONV_CH:2 * CONV_CH])
    dw = dw_ref[...]
    bias = bias_ref[...]
    gain = gain_ref[...]
    nbias = nbias_ref[...]
    lead = CONV_PAD - (CONV_WIDTH - 1)

    def body(c, carry):
        r0 = pl.multiple_of(c * CONV_ROWS, CONV_ROWS)
        win = g_ref[pl.ds(r0, CONV_ROWS + CONV_PAD), :]
        acc = jnp.zeros((CONV_ROWS, CONV_CH), F32) + bias
        for j in range(CONV_WIDTH):
            acc = acc + dw[j:j + 1, :] * win[lead + j:lead + j + CONV_ROWS, :]
        mu = jnp.mean(acc, axis=-1, keepdims=True)
        cen = acc - mu
        var = jnp.mean(cen * cen, axis=-1, keepdims=True)
        y = cen * lax.rsqrt(var + LN_EPS) * gain + nbias
        o_ref[pl.ds(r0, CONV_ROWS), :] = _silu(y)
        return carry

    lax.fori_loop(0, seq // CONV_ROWS, body, 0)


def _conv_mixer(u, dw, bias, gain, nbias):
    B, S, W = u.shape
    blk = lambda b: (b, 0, 0)
    const = lambda b: (0, 0)
    return pl.pallas_call(
        _conv_kernel,
        name="conformer_conv",
        grid=(B,),
        in_specs=[
            pl.BlockSpec((None, S, W), blk),
            pl.BlockSpec(dw.shape, const),
            pl.BlockSpec((1, CONV_CH), const),
            pl.BlockSpec((1, CONV_CH), const),
            pl.BlockSpec((1, CONV_CH), const),
        ],
        out_specs=pl.BlockSpec((None, S, CONV_CH), blk),
        out_shape=jax.ShapeDtypeStruct((B, S, CONV_CH), F32),
        scratch_shapes=[pltpu.VMEM((CONV_PAD + S, CONV_CH), F32)],
        compiler_params=pltpu.CompilerParams(
            dimension_semantics=("arbitrary",), vmem_limit_bytes=V7X_VMEM_LIMIT_BYTES),
    )(u, dw, bias, gain, nbias)


GDN_SEQ_BLOCK = 512
HIGHEST = lax.Precision.HIGHEST


def _unit_lower_inverse(m_strict, rows, cols):
    size = m_strict.shape[0]
    t = (rows == cols).astype(F32)
    step = 1
    while step < size:
        link = ((rows // step) == (cols // step) + 1) & ((rows // (2 * step)) == (cols // (2 * step)))
        c = jnp.where(link, m_strict, 0.0)
        t = t - _dot(t, _dot(c, t, HIGHEST), HIGHEST)
        step *= 2
    return t


def _gdn_kernel(qkv_ref, gate_ref, ab_ref, cw_ref, alog_ref, dtb_ref, onorm_ref, o_ref, state_ref, halo_ref):
    C = GDN_CHUNK
    D = GDN_HEAD_DIM
    W = GDN_WIDTH
    rows_blk = qkv_ref.shape[0]
    halo = halo_ref.shape[0]

    @pl.when(pl.program_id(1) == 0)
    def _():
        state_ref[...] = jnp.zeros_like(state_ref)
        halo_ref[...] = jnp.zeros_like(halo_ref)

    cw = cw_ref[...]
    alog = alog_ref[...]
    dtb = dtb_ref[...]
    onorm = onorm_ref[...]
    ri = lax.broadcasted_iota(jnp.int32, (C, C), 0)
    ci = lax.broadcasted_iota(jnp.int32, (C, C), 1)
    lower_incl = ri >= ci
    strict = ri > ci
    tril_ones = lower_incl.astype(F32)

    def chunk_body(c, carry):
        r0 = pl.multiple_of(c * C, C)
        prev_start = pl.multiple_of(jnp.maximum(r0 - halo, 0), halo)
        prev = jnp.where(c == 0, halo_ref[...], qkv_ref[pl.ds(prev_start, halo), :])
        win = jnp.concatenate([prev, qkv_ref[pl.ds(r0, C), :]], axis=0)
        lead = halo - (GDN_CONV_WIDTH - 1)
        acc = cw[0:1, :] * win[lead:lead + C, :]
        for j in range(1, GDN_CONV_WIDTH):
            acc = acc + cw[j:j + 1, :] * win[lead + j:lead + j + C, :]
        qkv = _silu(acc)

        ab = ab_ref[pl.ds(r0, C), :]
        xa = ab + dtb
        softplus = jnp.maximum(xa, 0.0) + jnp.log1p(jnp.exp(-jnp.abs(xa)))
        g = -jnp.exp(alog) * softplus
        gc = _dot(tril_ones, g, HIGHEST)
        gc_t = gc.T
        beta_all = _sigmoid(ab)
        gate = gate_ref[pl.ds(r0, C), :]

        for h in range(GDN_HEADS):
            q = qkv[:, h * D:(h + 1) * D]
            k = qkv[:, W + h * D:W + (h + 1) * D]
            v = qkv[:, 2 * W + h * D:2 * W + (h + 1) * D]
            q = q * lax.rsqrt(jnp.sum(q * q, axis=-1, keepdims=True) + L2_EPS) * (D ** -0.5)
            k = k * lax.rsqrt(jnp.sum(k * k, axis=-1, keepdims=True) + L2_EPS)
            g_col = gc[:, h:h + 1]
            g_row = gc_t[h:h + 1, :]
            beta = beta_all[:, GDN_HEADS + h:GDN_HEADS + h + 1]
            decay = jnp.exp(jnp.where(lower_incl, g_col - g_row, NEG_BIG))
            kb = k * beta
            vb = v * beta
            kbf = k.astype(BF16)
            m_strict = jnp.where(strict, _dot_nt(kb.astype(BF16), kbf) * decay, 0.0)
            t = _unit_lower_inverse(m_strict, ri, ci)
            rhs = jnp.concatenate([vb, kb * jnp.exp(g_col)], axis=1)
            sol = _dot(t, rhs, HIGHEST)
            u, w = sol[:, :D], sol[:, D:]
            intra = _dot_nt(q.astype(BF16), kbf) * decay
            state = state_ref[h]
            sb = state.astype(BF16)
            v_new = u - _dot(w.astype(BF16), sb)
            out = _dot((q * jnp.exp(g_col)).astype(BF16), sb) + _dot(intra.astype(BF16), v_new.astype(BF16))
            g_last = g_col[C - 1:C, :]
            k_dec = k * jnp.exp(g_last - g_col)
            state_ref[h] = state * jnp.exp(g_last) + _dot(k_dec.T.astype(BF16), v_new.astype(BF16))
            y = _rms(out) * onorm * _silu(gate[:, h * D:(h + 1) * D])
            o_ref[pl.ds(r0, C), h * D:(h + 1) * D] = y
        return carry

    lax.fori_loop(0, rows_blk // C, chunk_body, 0)
    halo_ref[...] = qkv_ref[rows_blk - halo:rows_blk, :]


def _gdn_mixer(qkv, gate, ab, conv_w, a_log, dt_bias, out_norm):
    B, S, _ = qkv.shape
    sb = GDN_SEQ_BLOCK
    blk = lambda b, s: (b, s, 0)
    const = lambda b, s: (0, 0)
    return pl.pallas_call(
        _gdn_kernel,
        name="gated_deltanet",
        grid=(B, S // sb),
        in_specs=[
            pl.BlockSpec((None, sb, 3 * GDN_WIDTH), blk),
            pl.BlockSpec((None, sb, GDN_WIDTH), blk),
            pl.BlockSpec((None, sb, V7X_LANES), blk),
            pl.BlockSpec(conv_w.shape, const),
            pl.BlockSpec((1, V7X_LANES), const),
            pl.BlockSpec((1, V7X_LANES), const),
            pl.BlockSpec((1, GDN_HEAD_DIM), const),
        ],
        out_specs=pl.BlockSpec((None, sb, GDN_WIDTH), blk),
        out_shape=jax.ShapeDtypeStruct((B, S, GDN_WIDTH), F32),
        scratch_shapes=[
            pltpu.VMEM((GDN_HEADS, GDN_HEAD_DIM, GDN_HEAD_DIM), F32),
            pltpu.VMEM((V7X_SUBLANES, 3 * GDN_WIDTH), F32),
        ],
        compiler_params=pltpu.CompilerParams(
            dimension_semantics=("arbitrary", "arbitrary"), vmem_limit_bytes=V7X_VMEM_LIMIT_BYTES),
    )(qkv, gate, ab, conv_w, a_log, dt_bias, out_norm)


def _out_ffn_kernel(x_ref, attn_ref, conv_ref, gdn_ref, wo_ref, g_post_ref, g_pre_ref, wg_ref, wu_ref, wd_ref,
                    g_ffn_ref, o_ref, *, hidden_chunk):
    a0, a1, a2 = ATTN_WIDTH, ATTN_WIDTH + CONV_CH, ATTN_WIDTH + CONV_CH + GDN_WIDTH
    y = _dot(attn_ref[...].astype(BF16), wo_ref[0:a0, :])
    y = y + _dot(conv_ref[...].astype(BF16), wo_ref[a0:a1, :])
    y = y + _dot(gdn_ref[...].astype(BF16), wo_ref[a1:a2, :])
    x1 = x_ref[...] + _rms(y) * g_post_ref[...]
    h = (_rms(x1) * g_pre_ref[...]).astype(BF16)
    hidden = wg_ref.shape[1]
    f = None
    for lo in range(0, hidden, hidden_chunk):
        hi = lo + hidden_chunk
        act = (_silu(_dot(h, wg_ref[:, lo:hi])) * _dot(h, wu_ref[:, lo:hi])).astype(BF16)
        part = _dot(act, wd_ref[lo:hi, :])
        f = part if f is None else f + part
    o_ref[...] = x1 + _rms(f) * g_ffn_ref[...]


def _out_ffn(x2d, attn, conv, gdn, wo, g_post, g_pre, wg, wu, wd, g_ffn, tm):
    T, D = x2d.shape
    hidden = wg.shape[1]
    hidden_chunk = hidden // 2
    row = lambda i: (i, 0)
    const = lambda i: (0, 0)
    resident = lambda shape: pl.BlockSpec(shape, const, pipeline_mode=pl.Buffered(1))
    return pl.pallas_call(
        functools.partial(_out_ffn_kernel, hidden_chunk=hidden_chunk),
        name="outproj_ffn",
        grid=(T // tm,),
        in_specs=[
            pl.BlockSpec((tm, D), row),
            pl.BlockSpec((tm, ATTN_WIDTH), row),
            pl.BlockSpec((tm, CONV_CH), row),
            pl.BlockSpec((tm, GDN_WIDTH), row),
            resident(wo.shape),
            pl.BlockSpec((1, D), const),
            pl.BlockSpec((1, D), const),
            resident(wg.shape),
            resident(wu.shape),
            resident(wd.shape),
            pl.BlockSpec((1, D), const),
        ],
        out_specs=pl.BlockSpec((tm, D), row),
        out_shape=jax.ShapeDtypeStruct((T, D), F32),
        compiler_params=pltpu.CompilerParams(
            dimension_semantics=("arbitrary",), vmem_limit_bytes=V7X_VMEM_LIMIT_BYTES),
    )(x2d, attn, conv, gdn, wo, g_post, g_pre, wg, wu, wd, g_ffn)


def _in_weight_columns():
    half = ATTN_HEAD_DIM // 2
    perm = np.empty(ATTN_WIDTH, np.int64)
    for h in range(ATTN_HEADS):
        for i in range(half):
            perm[h * half + i] = h * ATTN_HEAD_DIM + i
            perm[ATTN_WIDTH // 2 + h * half + i] = h * ATTN_HEAD_DIM + half + i
    in_width = 3 * ATTN_WIDTH + 2 * CONV_CH + 3 * GDN_WIDTH + GDN_WIDTH + 2 * GDN_HEADS
    return np.concatenate([perm, ATTN_WIDTH + perm, np.arange(2 * ATTN_WIDTH, in_width)])


def _rotary_tables(seq):
    half = ATTN_HEAD_DIM // 2
    inv_freq = jnp.exp(-math.log(ROPE_THETA) * jnp.arange(half, dtype=F32) * (2.0 / ATTN_HEAD_DIM))
    ang = jnp.arange(seq).astype(F32)[:, None] * inv_freq[None, :]
    return jnp.tile(jnp.cos(ang), (1, ATTN_HEADS)), jnp.tile(jnp.sin(ang), (1, ATTN_HEADS))


def _pad_lanes(t):
    return jnp.pad(t, ((0, 0), (0, V7X_LANES - t.shape[-1])))


def kernel(x, attn_pre_norm, w_in, conv_dw, conv_dw_bias, conv_norm_gain, conv_norm_bias, gdn_short_conv, gdn_a_log,
           gdn_dt_bias, gdn_out_norm, w_out, attn_post_norm, ffn_pre_norm, w_gate, w_up, w_down, ffn_post_norm):
    B, S, D = x.shape
    depth = w_in.shape[0]
    T = B * S
    tm = 512
    cols = _in_weight_columns()
    w_in_packed = jnp.pad(w_in[:, :, cols], ((0, 0), (0, 0), (0, W_TOTAL - cols.shape[0]))).astype(BF16)
    w_out_b, w_gate_b, w_up_b, w_down_b = (t.astype(BF16) for t in (w_out, w_gate, w_up, w_down))
    cos, sin = _rotary_tables(S)
    x2d = x.reshape(T, D)
    for i in range(depth):
        *attn_parts, conv_u, gdn_qkv, gdn_gate, gdn_ab = _inproj(
            x2d, attn_pre_norm[i][None, :], cos, sin, w_in_packed[i], S, tm)
        y_attn = _attention([t.reshape(B, S, -1) for t in attn_parts])
        y_conv = _conv_mixer(conv_u.reshape(B, S, -1), conv_dw[i], conv_dw_bias[i][None, :],
                             conv_norm_gain[i][None, :], conv_norm_bias[i][None, :])
        y_gdn = _gdn_mixer(gdn_qkv.reshape(B, S, -1), gdn_gate.reshape(B, S, -1), gdn_ab.reshape(B, S, -1),
                           gdn_short_conv[i], _pad_lanes(gdn_a_log[i][None, :]), _pad_lanes(gdn_dt_bias[i][None, :]),
                           gdn_out_norm[i][None, :])
        x2d = _out_ffn(x2d, y_attn.reshape(T, -1), y_conv.reshape(T, -1), y_gdn.reshape(T, -1), w_out_b[i],
                       attn_post_norm[i][None, :], ffn_pre_norm[i][None, :], w_gate_b[i], w_up_b[i], w_down_b[i],
                       ffn_post_norm[i][None, :], tm)
    return x2d.reshape(B, S, D)
```

```python
import functools
import math

import numpy as np
import jax
import jax.numpy as jnp
from jax import lax
from jax.experimental import pallas as pl
from jax.experimental.pallas import tpu as pltpu

F32 = jnp.float32
BF16 = jnp.bfloat16

ATTN_HEAD_DIM = 64
ATTN_HEADS = 4
ATTN_WIDTH = ATTN_HEADS * ATTN_HEAD_DIM
ATTN_BLOCK = 128
DILATIONS = (1, 4, 16)
ROPE_THETA = 10000.0
CONV_CH = 256
CONV_WIDTH = 31
GDN_HEADS = 4
GDN_HEAD_DIM = 128
GDN_WIDTH = GDN_HEADS * GDN_HEAD_DIM
GDN_CONV_WIDTH = 4
GDN_CHUNK = 128
RMS_EPS = 1e-6
LN_EPS = 1e-5
L2_EPS = 1e-6
NEG_BIG = -1e30

V7X_LANES = 128
V7X_SUBLANES = 8
V7X_VMEM_LIMIT_BYTES = 56 * 1024 * 1024

W_ATTN = 0
W_CONV = W_ATTN + 3 * ATTN_WIDTH
W_GQKV = W_CONV + 2 * CONV_CH
W_GGATE = W_GQKV + 3 * GDN_WIDTH
W_AB = W_GGATE + GDN_WIDTH
W_TOTAL = W_AB + V7X_LANES


def _rms(x):
    return x * lax.rsqrt(jnp.mean(x * x, axis=-1, keepdims=True) + RMS_EPS)


def _sigmoid(x):
    return 1.0 / (1.0 + jnp.exp(-x))


def _silu(x):
    return x * _sigmoid(x)


def _dot(a, b, precision=None):
    return jnp.dot(a, b, preferred_element_type=F32, precision=precision)


def _dot_nt(a, b, precision=None):
    return lax.dot_general(a, b, (((1,), (1,)), ((), ())), preferred_element_type=F32, precision=precision)


def _inproj_kernel(x_ref, gain_ref, cos_ref, sin_ref, w_ref, q1_ref, q2_ref, k1_ref, k2_ref, va_ref, vb_ref,
                   conv_ref, gqkv_ref, ggate_ref, ab_ref):
    h = (_rms(x_ref[...]) * gain_ref[...]).astype(BF16)

    def proj(lo, hi):
        return _dot(h, w_ref[:, lo:hi])

    qk = proj(W_ATTN, W_ATTN + 2 * ATTN_WIDTH)
    cos = cos_ref[...]
    sin = sin_ref[...]
    half = ATTN_WIDTH // 2
    q1, q2 = qk[:, 0:half], qk[:, half:2 * half]
    k1, k2 = qk[:, 2 * half:3 * half], qk[:, 3 * half:4 * half]
    scale = ATTN_HEAD_DIM ** -0.5
    q1_ref[...] = (q1 * cos - q2 * sin) * scale
    q2_ref[...] = (q2 * cos + q1 * sin) * scale
    k1_ref[...] = k1 * cos - k2 * sin
    k2_ref[...] = k2 * cos + k1 * sin
    v = proj(W_ATTN + 2 * ATTN_WIDTH, W_CONV)
    va_ref[...] = v[:, 0:half]
    vb_ref[...] = v[:, half:2 * half]
    conv_ref[...] = proj(W_CONV, W_GQKV)
    gqkv_ref[...] = proj(W_GQKV, W_GGATE)
    ggate_ref[...] = proj(W_GGATE, W_AB)
    ab_ref[...] = proj(W_AB, W_TOTAL)


def _inproj(x2d, gain, cos, sin, w, seq, tm):
    T, D = x2d.shape
    blocks_per_seq = seq // tm
    row = lambda i: (i, 0)
    const = lambda i: (0, 0)
    widths = (ATTN_WIDTH // 2,) * 6 + (2 * CONV_CH, 3 * GDN_WIDTH, GDN_WIDTH, V7X_LANES)
    return pl.pallas_call(
        _inproj_kernel,
        name="inproj",
        grid=(T // tm,),
        in_specs=[
            pl.BlockSpec((tm, D), row),
            pl.BlockSpec((1, D), const),
            pl.BlockSpec((tm, ATTN_WIDTH // 2), lambda i: (i % blocks_per_seq, 0)),
            pl.BlockSpec((tm, ATTN_WIDTH // 2), lambda i: (i % blocks_per_seq, 0)),
            pl.BlockSpec((D, W_TOTAL), const),
        ],
        out_specs=[pl.BlockSpec((tm, n), row) for n in widths],
        out_shape=[jax.ShapeDtypeStruct((T, n), F32) for n in widths],
        compiler_params=pltpu.CompilerParams(
            dimension_semantics=("arbitrary",), vmem_limit_bytes=V7X_VMEM_LIMIT_BYTES),
    )(x2d, gain, cos, sin, w)


def _attn_unit(q, k, v, valid):
    nq = q.shape[0]
    lane = lax.broadcasted_iota(jnp.int32, (1, ATTN_WIDTH), 1)
    qk_head = (lane % (ATTN_WIDTH // 2)) // (ATTN_HEAD_DIM // 2)
    v_head = lane // ATTN_HEAD_DIM
    qb = q.astype(BF16)
    zero = jnp.zeros_like(qb)
    q_stack = jnp.concatenate([jnp.where(qk_head == h, qb, zero) for h in range(ATTN_HEADS)], axis=0)
    s = _dot_nt(q_stack, k.astype(BF16))
    probs, inv_l, lse = [], [], []
    for h in range(ATTN_HEADS):
        s_h = jnp.where(valid, s[h * nq:(h + 1) * nq], NEG_BIG)
        m = jnp.max(s_h, axis=-1, keepdims=True)
        p = jnp.exp(s_h - m)
        l = jnp.sum(p, axis=-1, keepdims=True)
        probs.append(p.astype(BF16))
        inv_l.append(1.0 / l)
        lse.append(m + jnp.log(l))
    o = _dot(jnp.concatenate(probs, axis=0), v.astype(BF16))
    out = jnp.zeros((nq, ATTN_WIDTH), F32)
    lse_wide = jnp.zeros((nq, ATTN_WIDTH), F32)
    for h in range(ATTN_HEADS):
        sel = v_head == h
        out = jnp.where(sel, o[h * nq:(h + 1) * nq] * inv_l[h], out)
        lse_wide = jnp.where(sel, lse[h], lse_wide)
    return out, lse_wide


def _attn_kernel(q1_ref, q2_ref, k1_ref, k2_ref, va_ref, vb_ref, o_ref,
                 o2a_ref, o2b_ref, l2a_ref, l2b_ref, o3a_ref, o3b_ref, l3a_ref, l3b_ref):
    n = ATTN_BLOCK
    seq = q1_ref.shape[0]
    qi = lax.broadcasted_iota(jnp.int32, (n, 2 * n), 0)
    ki = lax.broadcasted_iota(jnp.int32, (n, 2 * n), 1)
    valid_band = (ki >= qi) & (ki <= qi + n)
    valid_first = lax.broadcasted_iota(jnp.int32, (n, n), 1) <= lax.broadcasted_iota(jnp.int32, (n, n), 0)

    def rows(start, size, stride):
        if stride == 1:
            return pl.ds(start, size)
        return pl.ds(start, size, stride=stride)

    def load(refs, sl):
        return jnp.concatenate([r[sl, :] for r in refs], axis=1)

    def store(refs, sl, val):
        for j, r in enumerate(refs):
            r[sl, :] = val[:, j * V7X_LANES:(j + 1) * V7X_LANES]

    def unit(q_start, stride, first):
        q = load((q1_ref, q2_ref), rows(q_start, n, stride))
        if first:
            k_rows = rows(q_start, n, stride)
            valid = valid_first
        else:
            k_rows = rows(q_start - n * stride, 2 * n, stride)
            valid = valid_band
        k = load((k1_ref, k2_ref), k_rows)
        v = load((va_ref, vb_ref), k_rows)
        return _attn_unit(q, k, v, valid)

    dilated = ((DILATIONS[1], (o2a_ref, o2b_ref), (l2a_ref, l2b_ref)),
               (DILATIONS[2], (o3a_ref, o3b_ref), (l3a_ref, l3b_ref)))
    for dil, o_refs, l_refs in dilated:
        span = n * dil
        nblk = seq // span

        def residue_body(r, carry, dil=dil, span=span, nblk=nblk, o_refs=o_refs, l_refs=l_refs):
            o, lse = unit(r, dil, True)
            store(o_refs, rows(r, n, dil), o)
            store(l_refs, rows(r, n, dil), lse)

            def blk_body(b, c):
                start = b * span + r
                o, lse = unit(start, dil, False)
                store(o_refs, rows(start, n, dil), o)
                store(l_refs, rows(start, n, dil), lse)
                return c

            if nblk > 1:
                lax.fori_loop(1, nblk, blk_body, 0)
            return carry

        lax.fori_loop(0, dil, residue_body, 0)

    def combine(start, o1, l1):
        sl = pl.ds(start, n)
        o2, l2 = load((o2a_ref, o2b_ref), sl), load((l2a_ref, l2b_ref), sl)
        o3, l3 = load((o3a_ref, o3b_ref), sl), load((l3a_ref, l3b_ref), sl)
        top = jnp.maximum(jnp.maximum(l1, l2), l3)
        w1, w2, w3 = jnp.exp(l1 - top), jnp.exp(l2 - top), jnp.exp(l3 - top)
        o_ref[sl, :] = (w1 * o1 + w2 * o2 + w3 * o3) / (w1 + w2 + w3)

    o1, l1 = unit(0, 1, True)
    combine(0, o1, l1)

    def dense_body(b, c):
        start = pl.multiple_of(b * n, n)
        o1, l1 = unit(start, 1, False)
        combine(start, o1, l1)
        return c

    lax.fori_loop(1, seq // n, dense_body, 0)


def _attention(parts):
    B, S, W = parts[0].shape
    blk = lambda b: (b, 0, 0)
    return pl.pallas_call(
        _attn_kernel,
        name="dilated_attn",
        grid=(B,),
        in_specs=[pl.BlockSpec((None, S, W), blk) for _ in parts],
        out_specs=pl.BlockSpec((None, S, ATTN_WIDTH), blk),
        out_shape=jax.ShapeDtypeStruct((B, S, ATTN_WIDTH), F32),
        scratch_shapes=[pltpu.VMEM((S, V7X_LANES), F32) for _ in range(8)],
        compiler_params=pltpu.CompilerParams(
            dimension_semantics=("arbitrary",), vmem_limit_bytes=V7X_VMEM_LIMIT_BYTES),
    )(*parts)


CONV_PAD = 32
CONV_ROWS = 64


def _conv_kernel(u_ref, dw_ref, bias_ref, gain_ref, nbias_ref, o_ref, g_ref):
    seq = u_ref.shape[0]
    g_ref[0:CONV_PAD, :] = jnp.zeros((CONV_PAD, CONV_CH), F32)
    g_ref[CONV_PAD:CONV_PAD + seq, :] = u_ref[:, 0:CONV_CH] * _sigmoid(u_ref[:, CONV_CH:2 * CONV_CH])
    dw = dw_ref[...]
    bias = bias_ref[...]
    gain = gain_ref[...]
    nbias = nbias_ref[...]
    lead = CONV_PAD - (CONV_WIDTH - 1)

    def body(c, carry):
        r0 = pl.multiple_of(c * CONV_ROWS, CONV_ROWS)
        win = g_ref[pl.ds(r0, CONV_ROWS + CONV_PAD), :]
        acc = jnp.zeros((CONV_ROWS, CONV_CH), F32) + bias
        for j in range(CONV_WIDTH):
            acc = acc + dw[j:j + 1, :] * win[lead + j:lead + j + CONV_ROWS, :]
        mu = jnp.mean(acc, axis=-1, keepdims=True)
        cen = acc - mu
        var = jnp.mean(cen * cen, axis=-1, keepdims=True)
        y = cen * lax.rsqrt(var + LN_EPS) * gain + nbias
        o_ref[pl.ds(r0, CONV_ROWS), :] = _silu(y)
        return carry

    lax.fori_loop(0, seq // CONV_ROWS, body, 0)


def _conv_mixer(u, dw, bias, gain, nbias):
    B, S, W = u.shape
    blk = lambda b: (b, 0, 0)
    const = lambda b: (0, 0)
    return pl.pallas_call(
        _conv_kernel,
        name="conformer_conv",
        grid=(B,),
        in_specs=[
            pl.BlockSpec((None, S, W), blk),
            pl.BlockSpec(dw.shape, const),
            pl.BlockSpec((1, CONV_CH), const),
            pl.BlockSpec((1, CONV_CH), const),
            pl.BlockSpec((1, CONV_CH), const),
        ],
        out_specs=pl.BlockSpec((None, S, CONV_CH), blk),
        out_shape=jax.ShapeDtypeStruct((B, S, CONV_CH), F32),
        scratch_shapes=[pltpu.VMEM((CONV_PAD + S, CONV_CH), F32)],
        compiler_params=pltpu.CompilerParams(
            dimension_semantics=("arbitrary",), vmem_limit_bytes=V7X_VMEM_LIMIT_BYTES),
    )(u, dw, bias, gain, nbias)


GDN_SEQ_BLOCK = 512
HIGHEST = lax.Precision.HIGHEST


def _unit_lower_inverses(m_list, rows, cols):
    size = m_list[0].shape[0]
    eye = (rows == cols).astype(F32)
    link = (rows // 2 == cols // 2) & (rows == cols + 1)
    ts = [eye - jnp.where(link, m, 0.0) for m in m_list]
    step = 2
    while step < size:
        link = ((rows // step) == (cols // step) + 1) & ((rows // (2 * step)) == (cols // (2 * step)))
        tbs = [t.astype(BF16) for t in ts]
        cts = [_dot(jnp.where(link, m, 0.0).astype(BF16), tb).astype(BF16) for m, tb in zip(m_list, tbs)]
        ts = [t - _dot(tb, ct) for t, tb, ct in zip(ts, tbs, cts)]
        step *= 2
    return ts


def _gdn_kernel(qkv_ref, gate_ref, ab_ref, cw_ref, alog_ref, dtb_ref, onorm_ref, o_ref, state_ref, halo_ref):
    C = GDN_CHUNK
    D = GDN_HEAD_DIM
    W = GDN_WIDTH
    rows_blk = qkv_ref.shape[0]
    halo = halo_ref.shape[0]

    @pl.when(pl.program_id(1) == 0)
    def _():
        state_ref[...] = jnp.zeros_like(state_ref)
        halo_ref[...] = jnp.zeros_like(halo_ref)

    cw = cw_ref[...]
    alog = alog_ref[...]
    dtb = dtb_ref[...]
    onorm = onorm_ref[...]
    ri = lax.broadcasted_iota(jnp.int32, (C, C), 0)
    ci = lax.broadcasted_iota(jnp.int32, (C, C), 1)
    lower_incl = ri >= ci
    strict = ri > ci
    tril_ones = lower_incl.astype(F32)

    lead = halo - (GDN_CONV_WIDTH - 1)

    chains = []
    for c in range(rows_blk // C):
        r0 = c * C
        prev = halo_ref[...] if c == 0 else qkv_ref[r0 - halo:r0, :]
        win = jnp.concatenate([prev, qkv_ref[r0:r0 + C, :]], axis=0)
        acc = cw[0:1, :] * win[lead:lead + C, :]
        for j in range(1, GDN_CONV_WIDTH):
            acc = acc + cw[j:j + 1, :] * win[lead + j:lead + j + C, :]
        qkv = _silu(acc)

        ab = ab_ref[r0:r0 + C, :]
        xa = ab + dtb
        softplus = jnp.maximum(xa, 0.0) + jnp.log1p(jnp.exp(-jnp.abs(xa)))
        g = -jnp.exp(alog) * softplus
        gc = _dot(tril_ones, g, HIGHEST)
        gc_t = gc.T
        beta_all = _sigmoid(ab)

        for h in range(GDN_HEADS):
            q = qkv[:, h * D:(h + 1) * D]
            k = qkv[:, W + h * D:W + (h + 1) * D]
            v = qkv[:, 2 * W + h * D:2 * W + (h + 1) * D]
            q = q * lax.rsqrt(jnp.sum(q * q, axis=-1, keepdims=True) + L2_EPS) * (D ** -0.5)
            k = k * lax.rsqrt(jnp.sum(k * k, axis=-1, keepdims=True) + L2_EPS)
            g_col = gc[:, h:h + 1]
            g_row = gc_t[h:h + 1, :]
            beta = beta_all[:, GDN_HEADS + h:GDN_HEADS + h + 1]
            decay = jnp.exp(jnp.where(lower_incl, g_col - g_row, NEG_BIG))
            kb = k * beta
            g_last = g_col[C - 1:C, :]
            chains.append(dict(
                decay=decay,
                kb=kb.astype(BF16),
                k=k.astype(BF16),
                q=q.astype(BF16),
                rhs=jnp.concatenate([v * beta, kb * jnp.exp(g_col)], axis=1).astype(BF16),
                q_dec=(q * jnp.exp(g_col)).astype(BF16),
                k_dec_t=(k * jnp.exp(g_last - g_col)).T.astype(BF16),
                carry=jnp.exp(g_last),
            ))

    m_list = [jnp.where(strict, _dot_nt(p["kb"], p["k"]) * p["decay"], 0.0) for p in chains]
    t_list = _unit_lower_inverses(m_list, ri, ci)
    sols = [_dot(t.astype(BF16), p["rhs"]) for t, p in zip(t_list, chains)]
    intras = [(_dot_nt(p["q"], p["k"]) * p["decay"]).astype(BF16) for p in chains]

    heads = range(GDN_HEADS)
    states = [state_ref[h] for h in heads]
    for c in range(rows_blk // C):
        r0 = c * C
        gate = gate_ref[r0:r0 + C, :]
        idx = [c * GDN_HEADS + h for h in heads]
        sbs = [states[h].astype(BF16) for h in heads]
        v_news = [(sols[i][:, :D] - _dot(sols[i][:, D:].astype(BF16), sbs[h])).astype(BF16) for h, i in zip(heads, idx)]
        outs = [_dot(chains[i]["q_dec"], sbs[h]) + _dot(intras[i], v_news[h]) for h, i in zip(heads, idx)]
        states = [states[h] * chains[i]["carry"] + _dot(chains[i]["k_dec_t"], v_news[h]) for h, i in zip(heads, idx)]
        for h in heads:
            y = _rms(outs[h]) * onorm * _silu(gate[:, h * D:(h + 1) * D])
            o_ref[r0:r0 + C, h * D:(h + 1) * D] = y
    for h in heads:
        state_ref[h] = states[h]
    halo_ref[...] = qkv_ref[rows_blk - halo:rows_blk, :]


def _gdn_mixer(qkv, gate, ab, conv_w, a_log, dt_bias, out_norm):
    B, S, _ = qkv.shape
    sb = GDN_SEQ_BLOCK
    blk = lambda b, s: (b, s, 0)
    const = lambda b, s: (0, 0)
    return pl.pallas_call(
        _gdn_kernel,
        name="gated_deltanet",
        grid=(B, S // sb),
        in_specs=[
            pl.BlockSpec((None, sb, 3 * GDN_WIDTH), blk),
            pl.BlockSpec((None, sb, GDN_WIDTH), blk),
            pl.BlockSpec((None, sb, V7X_LANES), blk),
            pl.BlockSpec(conv_w.shape, const),
            pl.BlockSpec((1, V7X_LANES), const),
            pl.BlockSpec((1, V7X_LANES), const),
            pl.BlockSpec((1, GDN_HEAD_DIM), const),
        ],
        out_specs=pl.BlockSpec((None, sb, GDN_WIDTH), blk),
        out_shape=jax.ShapeDtypeStruct((B, S, GDN_WIDTH), F32),
        scratch_shapes=[
            pltpu.VMEM((GDN_HEADS, GDN_HEAD_DIM, GDN_HEAD_DIM), F32),
            pltpu.VMEM((V7X_SUBLANES, 3 * GDN_WIDTH), F32),
        ],
        compiler_params=pltpu.CompilerParams(
            dimension_semantics=("arbitrary", "arbitrary"), vmem_limit_bytes=V7X_VMEM_LIMIT_BYTES),
    )(qkv, gate, ab, conv_w, a_log, dt_bias, out_norm)


def _out_ffn_kernel(x_ref, attn_ref, conv_ref, gdn_ref, wo_ref, g_post_ref, g_pre_ref, wg_ref, wu_ref, wd_ref,
                    g_ffn_ref, o_ref, *, hidden_chunk):
    a0, a1, a2 = ATTN_WIDTH, ATTN_WIDTH + CONV_CH, ATTN_WIDTH + CONV_CH + GDN_WIDTH
    y = _dot(attn_ref[...].astype(BF16), wo_ref[0:a0, :])
    y = y + _dot(conv_ref[...].astype(BF16), wo_ref[a0:a1, :])
    y = y + _dot(gdn_ref[...].astype(BF16), wo_ref[a1:a2, :])
    x1 = x_ref[...] + _rms(y) * g_post_ref[...]
    h = (_rms(x1) * g_pre_ref[...]).astype(BF16)
    hidden = wg_ref.shape[1]
    f = None
    for lo in range(0, hidden, hidden_chunk):
        hi = lo + hidden_chunk
        act = (_silu(_dot(h, wg_ref[:, lo:hi])) * _dot(h, wu_ref[:, lo:hi])).astype(BF16)
        part = _dot(act, wd_ref[lo:hi, :])
        f = part if f is None else f + part
    o_ref[...] = x1 + _rms(f) * g_ffn_ref[...]


def _out_ffn(x2d, attn, conv, gdn, wo, g_post, g_pre, wg, wu, wd, g_ffn, tm):
    T, D = x2d.shape
    hidden = wg.shape[1]
    hidden_chunk = hidden // 2
    row = lambda i: (i, 0)
    const = lambda i: (0, 0)
    resident = lambda shape: pl.BlockSpec(shape, const, pipeline_mode=pl.Buffered(1))
    return pl.pallas_call(
        functools.partial(_out_ffn_kernel, hidden_chunk=hidden_chunk),
        name="outproj_ffn",
        grid=(T // tm,),
        in_specs=[
            pl.BlockSpec((tm, D), row),
            pl.BlockSpec((tm, ATTN_WIDTH), row),
            pl.BlockSpec((tm, CONV_CH), row),
            pl.BlockSpec((tm, GDN_WIDTH), row),
            resident(wo.shape),
            pl.BlockSpec((1, D), const),
            pl.BlockSpec((1, D), const),
            resident(wg.shape),
            resident(wu.shape),
            resident(wd.shape),
            pl.BlockSpec((1, D), const),
        ],
        out_specs=pl.BlockSpec((tm, D), row),
        out_shape=jax.ShapeDtypeStruct((T, D), F32),
        compiler_params=pltpu.CompilerParams(
            dimension_semantics=("arbitrary",), vmem_limit_bytes=V7X_VMEM_LIMIT_BYTES),
    )(x2d, attn, conv, gdn, wo, g_post, g_pre, wg, wu, wd, g_ffn)


def _in_weight_columns():
    half = ATTN_HEAD_DIM // 2
    perm = np.empty(ATTN_WIDTH, np.int64)
    for h in range(ATTN_HEADS):
        for i in range(half):
            perm[h * half + i] = h * ATTN_HEAD_DIM + i
            perm[ATTN_WIDTH // 2 + h * half + i] = h * ATTN_HEAD_DIM + half + i
    in_width = 3 * ATTN_WIDTH + 2 * CONV_CH + 3 * GDN_WIDTH + GDN_WIDTH + 2 * GDN_HEADS
    return np.concatenate([perm, ATTN_WIDTH + perm, np.arange(2 * ATTN_WIDTH, in_width)])


def _rotary_tables(seq):
    half = ATTN_HEAD_DIM // 2
    inv_freq = jnp.exp(-math.log(ROPE_THETA) * jnp.arange(half, dtype=F32) * (2.0 / ATTN_HEAD_DIM))
    ang = jnp.arange(seq).astype(F32)[:, None] * inv_freq[None, :]
    return jnp.tile(jnp.cos(ang), (1, ATTN_HEADS)), jnp.tile(jnp.sin(ang), (1, ATTN_HEADS))


def _pad_lanes(t):
    return jnp.pad(t, ((0, 0), (0, V7X_LANES - t.shape[-1])))


def kernel(x, attn_pre_norm, w_in, conv_dw, conv_dw_bias, conv_norm_gain, conv_norm_bias, gdn_short_conv, gdn_a_log,
           gdn_dt_bias, gdn_out_norm, w_out, attn_post_norm, ffn_pre_norm, w_gate, w_up, w_down, ffn_post_norm):
    B, S, D = x.shape
    depth = w_in.shape[0]
    T = B * S
    tm = 512
    cols = _in_weight_columns()
    w_in_packed = jnp.pad(w_in[:, :, cols], ((0, 0), (0, 0), (0, W_TOTAL - cols.shape[0]))).astype(BF16)
    w_out_b, w_gate_b, w_up_b, w_down_b = (t.astype(BF16) for t in (w_out, w_gate, w_up, w_down))
    cos, sin = _rotary_tables(S)
    x2d = x.reshape(T, D)
    for i in range(depth):
        *attn_parts, conv_u, gdn_qkv, gdn_gate, gdn_ab = _inproj(
            x2d, attn_pre_norm[i][None, :], cos, sin, w_in_packed[i], S, tm)
        y_attn = _attention([t.reshape(B, S, -1) for t in attn_parts])
        y_conv = _conv_mixer(conv_u.reshape(B, S, -1), conv_dw[i], conv_dw_bias[i][None, :],
                             conv_norm_gain[i][None, :], conv_norm_bias[i][None, :])
        y_gdn = _gdn_mixer(gdn_qkv.reshape(B, S, -1), gdn_gate.reshape(B, S, -1), gdn_ab.reshape(B, S, -1),
                           gdn_short_conv[i], _pad_lanes(gdn_a_log[i][None, :]), _pad_lanes(gdn_dt_bias[i][None, :]),
                           gdn_out_norm[i][None, :])
        x2d = _out_ffn(x2d, y_attn.reshape(T, -1), y_conv.reshape(T, -1), y_gdn.reshape(T, -1), w_out_b[i],
                       attn_post_norm[i][None, :], ffn_pre_norm[i][None, :], w_gate_b[i], w_up_b[i], w_down_b[i],
                       ffn_post_norm[i][None, :], tm)
    return x2d.reshape(B, S, D)
```

```python
import functools
import math

import jax
import jax.numpy as jnp
from jax import lax
from jax.experimental import pallas as pl
from jax.experimental.pallas import tpu as pltpu

F32 = jnp.float32
BF16 = jnp.bfloat16

ATTN_HEAD_DIM = 64
ATTN_HEADS = 4
ATTN_WIDTH = ATTN_HEADS * ATTN_HEAD_DIM
ATTN_BLOCK = 128
ATTN_GROUP = 4
DILATIONS = (1, 4, 16)
ROPE_THETA = 10000.0
CONV_CH = 256
CONV_WIDTH = 31
GDN_HEADS = 4
GDN_HEAD_DIM = 128
GDN_WIDTH = GDN_HEADS * GDN_HEAD_DIM
GDN_CONV_WIDTH = 4
GDN_CHUNK = 128
RMS_EPS = 1e-6
LN_EPS = 1e-5
L2_EPS = 1e-6
NEG_BIG = -1e30

V7X_LANES = 128
V7X_SUBLANES = 8
V7X_VMEM_LIMIT_BYTES = 56 * 1024 * 1024

W_ATTN = 0
W_CONV = W_ATTN + 3 * ATTN_WIDTH
W_GQKV = W_CONV + 2 * CONV_CH
W_GGATE = W_GQKV + 3 * GDN_WIDTH
W_AB = W_GGATE + GDN_WIDTH
W_TOTAL = W_AB + V7X_LANES


def _rms(x):
    return x * lax.rsqrt(jnp.mean(x * x, axis=-1, keepdims=True) + RMS_EPS)


def _sigmoid(x):
    return 1.0 / (1.0 + jnp.exp(-x))


def _silu(x):
    return x * _sigmoid(x)


def _dot(a, b, precision=None):
    return jnp.dot(a, b, preferred_element_type=F32, precision=precision)


def _dot_nt(a, b, precision=None):
    return lax.dot_general(a, b, (((1,), (1,)), ((), ())), preferred_element_type=F32, precision=precision)


def _inproj_kernel(x_ref, gain_ref, cos_ref, sin_ref, w_ref, q1_ref, q2_ref, k1_ref, k2_ref, va_ref, vb_ref,
                   conv_ref, gqkv_ref, ggate_ref, ab_ref):
    h = (_rms(x_ref[...]) * gain_ref[...]).astype(BF16)

    def proj(lo, hi):
        return _dot(h, w_ref[:, lo:hi])

    qk = proj(W_ATTN, W_ATTN + 2 * ATTN_WIDTH)
    cos = cos_ref[...]
    sin = sin_ref[...]
    half = ATTN_WIDTH // 2
    q1, q2 = qk[:, 0:half], qk[:, half:2 * half]
    k1, k2 = qk[:, 2 * half:3 * half], qk[:, 3 * half:4 * half]
    scale = ATTN_HEAD_DIM ** -0.5
    q1_ref[...] = (q1 * cos - q2 * sin) * scale
    q2_ref[...] = (q2 * cos + q1 * sin) * scale
    k1_ref[...] = k1 * cos - k2 * sin
    k2_ref[...] = k2 * cos + k1 * sin
    v = proj(W_ATTN + 2 * ATTN_WIDTH, W_CONV)
    va_ref[...] = v[:, 0:half]
    vb_ref[...] = v[:, half:2 * half]
    conv_ref[...] = proj(W_CONV, W_GQKV)
    gqkv_ref[...] = proj(W_GQKV, W_GGATE)
    ggate_ref[...] = proj(W_GGATE, W_AB)
    ab_ref[...] = proj(W_AB, W_TOTAL)


def _inproj(x2d, gain, cos, sin, w, seq, tm):
    T, D = x2d.shape
    blocks_per_seq = seq // tm
    row = lambda i: (i, 0)
    const = lambda i: (0, 0)
    widths = (ATTN_WIDTH // 2,) * 6 + (2 * CONV_CH, 3 * GDN_WIDTH, GDN_WIDTH, V7X_LANES)
    return pl.pallas_call(
        _inproj_kernel,
        name="inproj",
        grid=(T // tm,),
        in_specs=[
            pl.BlockSpec((tm, D), row),
            pl.BlockSpec((1, D), const),
            pl.BlockSpec((tm, ATTN_WIDTH // 2), lambda i: (i % blocks_per_seq, 0)),
            pl.BlockSpec((tm, ATTN_WIDTH // 2), lambda i: (i % blocks_per_seq, 0)),
            pl.BlockSpec((D, W_TOTAL), const),
        ],
        out_specs=[pl.BlockSpec((tm, n), row) for n in widths],
        out_shape=[jax.ShapeDtypeStruct((T, n), F32) for n in widths],
        compiler_params=pltpu.CompilerParams(
            dimension_semantics=("arbitrary",), vmem_limit_bytes=V7X_VMEM_LIMIT_BYTES),
    )(x2d, gain, cos, sin, w)


def _attn_units(units):
    heads = range(ATTN_HEADS)
    lane = lax.broadcasted_iota(jnp.int32, (1, ATTN_WIDTH), 1)
    qk_head = (lane % (ATTN_WIDTH // 2)) // (ATTN_HEAD_DIM // 2)
    v_head = lane // ATTN_HEAD_DIM
    scores = []
    for q, k, _, _ in units:
        qb = q.astype(BF16)
        zero = jnp.zeros_like(qb)
        q_stack = jnp.concatenate([jnp.where(qk_head == h, qb, zero) for h in heads], axis=0)
        scores.append(_dot_nt(q_stack, k.astype(BF16)))
    stats = []
    for s, (q, _, _, valid) in zip(scores, units):
        nq = q.shape[0]
        probs, inv_l, lse = [], [], []
        for h in heads:
            s_h = jnp.where(valid, s[h * nq:(h + 1) * nq], NEG_BIG)
            m = jnp.max(s_h, axis=-1, keepdims=True)
            p = jnp.exp(s_h - m)
            l = jnp.sum(p, axis=-1, keepdims=True)
            probs.append(p.astype(BF16))
            inv_l.append(1.0 / l)
            lse.append(m + jnp.log(l))
        stats.append((jnp.concatenate(probs, axis=0), inv_l, lse))
    outs = [_dot(p, v.astype(BF16)) for (p, _, _), (_, _, v, _) in zip(stats, units)]
    results = []
    for o, (_, inv_l, lse), (q, _, _, _) in zip(outs, stats, units):
        nq = q.shape[0]
        out = jnp.zeros((nq, ATTN_WIDTH), F32)
        lse_wide = jnp.zeros((nq, ATTN_WIDTH), F32)
        for h in heads:
            sel = v_head == h
            out = jnp.where(sel, o[h * nq:(h + 1) * nq] * inv_l[h], out)
            lse_wide = jnp.where(sel, lse[h], lse_wide)
        results.append((out, lse_wide))
    return results


def _attn_kernel(q1_ref, q2_ref, k1_ref, k2_ref, va_ref, vb_ref, o_ref,
                 o2a_ref, o2b_ref, l2a_ref, l2b_ref, o3a_ref, o3b_ref, l3a_ref, l3b_ref):
    n = ATTN_BLOCK
    seq = q1_ref.shape[0]
    qi = lax.broadcasted_iota(jnp.int32, (n, 2 * n), 0)
    ki = lax.broadcasted_iota(jnp.int32, (n, 2 * n), 1)
    valid_band = (ki >= qi) & (ki <= qi + n)
    valid_first = lax.broadcasted_iota(jnp.int32, (n, n), 1) <= lax.broadcasted_iota(jnp.int32, (n, n), 0)

    def rows(start, size, stride):
        if stride == 1:
            return pl.ds(start, size)
        return pl.ds(start, size, stride=stride)

    def load(refs, sl):
        return jnp.concatenate([r[sl, :] for r in refs], axis=1)

    def store(refs, sl, val):
        for j, r in enumerate(refs):
            r[sl, :] = val[:, j * V7X_LANES:(j + 1) * V7X_LANES]

    def run_units(specs):
        units = []
        for q_start, stride, first in specs:
            q = load((q1_ref, q2_ref), rows(q_start, n, stride))
            if first:
                k_rows = rows(q_start, n, stride)
                valid = valid_first
            else:
                k_rows = rows(q_start - n * stride, 2 * n, stride)
                valid = valid_band
            units.append((q, load((k1_ref, k2_ref), k_rows), load((va_ref, vb_ref), k_rows), valid))
        return _attn_units(units)

    def run_dilated(specs, o_refs, l_refs):
        for (q_start, stride, _), (o, lse) in zip(specs, run_units(specs)):
            store(o_refs, rows(q_start, n, stride), o)
            store(l_refs, rows(q_start, n, stride), lse)

    d2, d3 = DILATIONS[1], DILATIONS[2]
    assert seq == n * d3 and seq == ATTN_GROUP * n * d2 and d3 % ATTN_GROUP == 0

    def widest_body(g, carry):
        specs = [(g * ATTN_GROUP + u, d3, True) for u in range(ATTN_GROUP)]
        run_dilated(specs, (o3a_ref, o3b_ref), (l3a_ref, l3b_ref))
        return carry

    lax.fori_loop(0, d3 // ATTN_GROUP, widest_body, 0)

    def middle_body(r, carry):
        specs = [(b * n * d2 + r, d2, b == 0) for b in range(ATTN_GROUP)]
        run_dilated(specs, (o2a_ref, o2b_ref), (l2a_ref, l2b_ref))
        return carry

    lax.fori_loop(0, d2, middle_body, 0)

    def run_dense(specs):
        for (start, _, _), (o1, l1) in zip(specs, run_units(specs)):
            sl = pl.ds(start, n)
            o2, l2 = load((o2a_ref, o2b_ref), sl), load((l2a_ref, l2b_ref), sl)
            o3, l3 = load((o3a_ref, o3b_ref), sl), load((l3a_ref, l3b_ref), sl)
            top = jnp.maximum(jnp.maximum(l1, l2), l3)
            w1, w2, w3 = jnp.exp(l1 - top), jnp.exp(l2 - top), jnp.exp(l3 - top)
            o_ref[sl, :] = (w1 * o1 + w2 * o2 + w3 * o3) / (w1 + w2 + w3)

    run_dense([(u * n, 1, u == 0) for u in range(ATTN_GROUP)])

    def dense_body(g, carry):
        base = pl.multiple_of(g * (ATTN_GROUP * n), ATTN_GROUP * n)
        run_dense([(base + u * n, 1, False) for u in range(ATTN_GROUP)])
        return carry

    lax.fori_loop(1, seq // (ATTN_GROUP * n), dense_body, 0)


def _attention(parts):
    B, S, W = parts[0].shape
    blk = lambda b: (b, 0, 0)
    return pl.pallas_call(
        _attn_kernel,
        name="dilated_attn",
        grid=(B,),
        in_specs=[pl.BlockSpec((None, S, W), blk) for _ in parts],
        out_specs=pl.BlockSpec((None, S, ATTN_WIDTH), blk),
        out_shape=jax.ShapeDtypeStruct((B, S, ATTN_WIDTH), F32),
        scratch_shapes=[pltpu.VMEM((S, V7X_LANES), F32) for _ in range(8)],
        compiler_params=pltpu.CompilerParams(
            dimension_semantics=("arbitrary",), vmem_limit_bytes=V7X_VMEM_LIMIT_BYTES),
    )(*parts)


CONV_PAD = 32
CONV_ROWS = 128


def _conv_kernel(u_ref, dw_ref, bias_ref, gain_ref, nbias_ref, o_ref, g_ref):
    seq = u_ref.shape[0]
    g_ref[0:CONV_PAD, :] = jnp.zeros((CONV_PAD, CONV_CH), F32)
    g_ref[CONV_PAD:CONV_PAD + seq, :] = u_ref[:, 0:CONV_CH] * _sigmoid(u_ref[:, CONV_CH:2 * CONV_CH])
    dw = dw_ref[...]
    bias = bias_ref[...]
    gain = gain_ref[...]
    nbias = nbias_ref[...]
    lead = CONV_PAD - (CONV_WIDTH - 1)
    sub = V7X_SUBLANES

    def body(c, carry):
        r0 = pl.multiple_of(c * CONV_ROWS, CONV_ROWS)
        acc = jnp.zeros((CONV_ROWS, CONV_CH), F32) + bias
        for rem in range(sub):
            taps = [j for j in range(CONV_WIDTH) if (lead + j) % sub == rem]
            nrows = CONV_ROWS if rem == 0 else CONV_ROWS + sub
            part = None
            for j in taps:
                start = pl.multiple_of(r0 + (lead + j - rem), sub)
                term = dw[j:j + 1, :] * g_ref[pl.ds(start, nrows), :]
                part = term if part is None else part + term
            if rem:
                part = pltpu.roll(part, nrows - rem, axis=0)[0:CONV_ROWS, :]
            acc = acc + part
        mu = jnp.mean(acc, axis=-1, keepdims=True)
        cen = acc - mu
        var = jnp.mean(cen * cen, axis=-1, keepdims=True)
        y = cen * lax.rsqrt(var + LN_EPS) * gain + nbias
        o_ref[pl.ds(r0, CONV_ROWS), :] = _silu(y)
        return carry

    lax.fori_loop(0, seq // CONV_ROWS, body, 0)


def _conv_mixer(u, dw, bias, gain, nbias):
    B, S, W = u.shape
    blk = lambda b: (b, 0, 0)
    const = lambda b: (0, 0)
    return pl.pallas_call(
        _conv_kernel,
        name="conformer_conv",
        grid=(B,),
        in_specs=[
            pl.BlockSpec((None, S, W), blk),
            pl.BlockSpec(dw.shape, const),
            pl.BlockSpec((1, CONV_CH), const),
            pl.BlockSpec((1, CONV_CH), const),
            pl.BlockSpec((1, CONV_CH), const),
        ],
        out_specs=pl.BlockSpec((None, S, CONV_CH), blk),
        out_shape=jax.ShapeDtypeStruct((B, S, CONV_CH), F32),
        scratch_shapes=[pltpu.VMEM((CONV_PAD + S, CONV_CH), F32)],
        compiler_params=pltpu.CompilerParams(
            dimension_semantics=("arbitrary",), vmem_limit_bytes=V7X_VMEM_LIMIT_BYTES),
    )(u, dw, bias, gain, nbias)


GDN_SEQ_BLOCK = 512
HIGHEST = lax.Precision.HIGHEST


def _unit_lower_inverses(m_list, rows, cols):
    size = m_list[0].shape[0]
    eye = (rows == cols).astype(F32)
    link = (rows // 2 == cols // 2) & (rows == cols + 1)
    ts = [eye - jnp.where(link, m, 0.0) for m in m_list]
    neg_mbs = [(-m).astype(BF16) for m in m_list]
    step = 2
    while step < size:
        link = ((rows // step) == (cols // step) + 1) & ((rows // (2 * step)) == (cols // (2 * step)))
        tbs = [t.astype(BF16) for t in ts]
        mts = [_dot(mb, tb).astype(BF16) for mb, tb in zip(neg_mbs, tbs)]
        ts = [jnp.where(link, _dot(tb, mt), t) for t, tb, mt in zip(ts, tbs, mts)]
        step *= 2
    return ts


def _gdn_kernel(qkv_ref, gate_ref, ab_ref, cw_ref, alog_ref, dtb_ref, onorm_ref, o_ref, state_ref, halo_ref):
    C = GDN_CHUNK
    D = GDN_HEAD_DIM
    W = GDN_WIDTH
    rows_blk = qkv_ref.shape[0]
    halo = halo_ref.shape[0]

    @pl.when(pl.program_id(1) == 0)
    def _():
        state_ref[...] = jnp.zeros_like(state_ref)
        halo_ref[...] = jnp.zeros_like(halo_ref)

    cw = cw_ref[...]
    alog = alog_ref[...]
    dtb = dtb_ref[...]
    onorm = onorm_ref[...]
    ri = lax.broadcasted_iota(jnp.int32, (C, C), 0)
    ci = lax.broadcasted_iota(jnp.int32, (C, C), 1)
    lower_incl = ri >= ci
    strict = ri > ci
    tril_ones = lower_incl.astype(F32)

    lead = halo - (GDN_CONV_WIDTH - 1)

    def prepare(c, chains):
        r0 = c * C
        prev = halo_ref[...] if c == 0 else qkv_ref[r0 - halo:r0, :]
        win = jnp.concatenate([prev, qkv_ref[r0:r0 + C, :]], axis=0)
        last = GDN_CONV_WIDTH - 1
        acc = cw[last:last + 1, :] * win[halo:, :]
        for j in range(last):
            acc = acc + cw[j:j + 1, :] * pltpu.roll(win, last - j, axis=0)[halo:, :]
        qkv = _silu(acc)

        ab = ab_ref[r0:r0 + C, :]
        xa = ab + dtb
        softplus = jnp.maximum(xa, 0.0) + jnp.log1p(jnp.exp(-jnp.abs(xa)))
        g = -jnp.exp(alog) * softplus
        gc = _dot(tril_ones, g, HIGHEST)
        gc_t = gc.T
        beta_all = _sigmoid(ab)

        for h in range(GDN_HEADS):
            q = qkv[:, h * D:(h + 1) * D]
            k = qkv[:, W + h * D:W + (h + 1) * D]
            v = qkv[:, 2 * W + h * D:2 * W + (h + 1) * D]
            q = q * lax.rsqrt(jnp.sum(q * q, axis=-1, keepdims=True) + L2_EPS) * (D ** -0.5)
            k = k * lax.rsqrt(jnp.sum(k * k, axis=-1, keepdims=True) + L2_EPS)
            g_col = gc[:, h:h + 1]
            g_row = gc_t[h:h + 1, :]
            beta = beta_all[:, GDN_HEADS + h:GDN_HEADS + h + 1]
            decay = jnp.exp(jnp.where(lower_incl, g_col - g_row, NEG_BIG))
            kb = k * beta
            g_last = g_col[C - 1:C, :]
            chains.append(dict(
                decay=decay,
                kb=kb.astype(BF16),
                k=k.astype(BF16),
                q=q.astype(BF16),
                rhs=jnp.concatenate([v * beta, kb * jnp.exp(g_col)], axis=1).astype(BF16),
                q_dec=(q * jnp.exp(g_col)).astype(BF16),
                k_dec_t=(k * jnp.exp(g_last - g_col)).T.astype(BF16),
                carry=jnp.exp(g_last),
            ))

    nchunks = rows_blk // C
    chains, sols, intras = [], [], []
    for group in (range(nchunks),):
        part = []
        for c in group:
            prepare(c, part)
        m_list = [jnp.where(strict, _dot_nt(p["kb"], p["k"]) * p["decay"], 0.0) for p in part]
        t_list = _unit_lower_inverses(m_list, ri, ci)
        sols += [_dot(t.astype(BF16), p["rhs"]) for t, p in zip(t_list, part)]
        intras += [(_dot_nt(p["q"], p["k"]) * p["decay"]).astype(BF16) for p in part]
        chains += part

    heads = range(GDN_HEADS)
    states = [state_ref[h] for h in heads]
    for c in range(rows_blk // C):
        r0 = c * C
        gate = gate_ref[r0:r0 + C, :]
        idx = [c * GDN_HEADS + h for h in heads]
        sbs = [states[h].astype(BF16) for h in heads]
        v_news = [(sols[i][:, :D] - _dot(sols[i][:, D:].astype(BF16), sbs[h])).astype(BF16) for h, i in zip(heads, idx)]
        outs = [_dot(chains[i]["q_dec"], sbs[h]) + _dot(intras[i], v_news[h]) for h, i in zip(heads, idx)]
        states = [states[h] * chains[i]["carry"] + _dot(chains[i]["k_dec_t"], v_news[h]) for h, i in zip(heads, idx)]
        for h in heads:
            y = _rms(outs[h]) * onorm * _silu(gate[:, h * D:(h + 1) * D])
            o_ref[r0:r0 + C, h * D:(h + 1) * D] = y
    for h in heads:
        state_ref[h] = states[h]
    halo_ref[...] = qkv_ref[rows_blk - halo:rows_blk, :]


def _gdn_mixer(qkv, gate, ab, conv_w, a_log, dt_bias, out_norm):
    B, S, _ = qkv.shape
    sb = GDN_SEQ_BLOCK
    blk = lambda b, s: (b, s, 0)
    const = lambda b, s: (0, 0)
    return pl.pallas_call(
        _gdn_kernel,
        name="gated_deltanet",
        grid=(B, S // sb),
        in_specs=[
            pl.BlockSpec((None, sb, 3 * GDN_WIDTH), blk),
            pl.BlockSpec((None, sb, GDN_WIDTH), blk),
            pl.BlockSpec((None, sb, V7X_LANES), blk),
            pl.BlockSpec(conv_w.shape, const),
            pl.BlockSpec((1, V7X_LANES), const),
            pl.BlockSpec((1, V7X_LANES), const),
            pl.BlockSpec((1, GDN_HEAD_DIM), const),
        ],
        out_specs=pl.BlockSpec((None, sb, GDN_WIDTH), blk),
        out_shape=jax.ShapeDtypeStruct((B, S, GDN_WIDTH), F32),
        scratch_shapes=[
            pltpu.VMEM((GDN_HEADS, GDN_HEAD_DIM, GDN_HEAD_DIM), F32),
            pltpu.VMEM((V7X_SUBLANES, 3 * GDN_WIDTH), F32),
        ],
        compiler_params=pltpu.CompilerParams(
            dimension_semantics=("arbitrary", "arbitrary"), vmem_limit_bytes=V7X_VMEM_LIMIT_BYTES),
    )(qkv, gate, ab, conv_w, a_log, dt_bias, out_norm)


def _out_ffn_kernel(x_ref, attn_ref, conv_ref, gdn_ref, wo_ref, g_post_ref, g_pre_ref, wg_ref, wu_ref, wd_ref,
                    g_ffn_ref, o_ref, *, hidden_chunk):
    a0, a1, a2 = ATTN_WIDTH, ATTN_WIDTH + CONV_CH, ATTN_WIDTH + CONV_CH + GDN_WIDTH
    y = _dot(attn_ref[...].astype(BF16), wo_ref[0:a0, :])
    y = y + _dot(conv_ref[...].astype(BF16), wo_ref[a0:a1, :])
    y = y + _dot(gdn_ref[...].astype(BF16), wo_ref[a1:a2, :])
    x1 = x_ref[...] + _rms(y) * g_post_ref[...]
    h = (_rms(x1) * g_pre_ref[...]).astype(BF16)
    hidden = wg_ref.shape[1]
    f = None
    for lo in range(0, hidden, hidden_chunk):
        hi = lo + hidden_chunk
        act = (_silu(_dot(h, wg_ref[:, lo:hi])) * _dot(h, wu_ref[:, lo:hi])).astype(BF16)
        part = _dot(act, wd_ref[lo:hi, :])
        f = part if f is None else f + part
    o_ref[...] = x1 + _rms(f) * g_ffn_ref[...]


def _out_ffn(x2d, attn, conv, gdn, wo, g_post, g_pre, wg, wu, wd, g_ffn, tm):
    T, D = x2d.shape
    hidden = wg.shape[1]
    hidden_chunk = hidden // 2
    row = lambda i: (i, 0)
    const = lambda i: (0, 0)
    resident = lambda shape: pl.BlockSpec(shape, const, pipeline_mode=pl.Buffered(1))
    return pl.pallas_call(
        functools.partial(_out_ffn_kernel, hidden_chunk=hidden_chunk),
        name="outproj_ffn",
        grid=(T // tm,),
        in_specs=[
            pl.BlockSpec((tm, D), row),
            pl.BlockSpec((tm, ATTN_WIDTH), row),
            pl.BlockSpec((tm, CONV_CH), row),
            pl.BlockSpec((tm, GDN_WIDTH), row),
            resident(wo.shape),
            pl.BlockSpec((1, D), const),
            pl.BlockSpec((1, D), const),
            resident(wg.shape),
            resident(wu.shape),
            resident(wd.shape),
            pl.BlockSpec((1, D), const),
        ],
        out_specs=pl.BlockSpec((tm, D), row),
        out_shape=jax.ShapeDtypeStruct((T, D), F32),
        compiler_params=pltpu.CompilerParams(
            dimension_semantics=("arbitrary",), vmem_limit_bytes=V7X_VMEM_LIMIT_BYTES),
    )(x2d, attn, conv, gdn, wo, g_post, g_pre, wg, wu, wd, g_ffn)


def _pack_in_weight(w_in):
    depth, d_model, in_width = w_in.shape
    wb = w_in.astype(BF16)
    half = ATTN_HEAD_DIM // 2
    qk = wb[:, :, :2 * ATTN_WIDTH].reshape(depth, d_model, 2, ATTN_HEADS, 2, half)
    qk = qk.transpose(0, 1, 2, 4, 3, 5).reshape(depth, d_model, 2 * ATTN_WIDTH)
    pad = jnp.zeros((depth, d_model, W_TOTAL - in_width), BF16)
    return jnp.concatenate([qk, wb[:, :, 2 * ATTN_WIDTH:], pad], axis=-1)


def _rotary_tables(seq):
    half = ATTN_HEAD_DIM // 2
    inv_freq = jnp.exp(-math.log(ROPE_THETA) * jnp.arange(half, dtype=F32) * (2.0 / ATTN_HEAD_DIM))
    ang = jnp.arange(seq).astype(F32)[:, None] * inv_freq[None, :]
    return jnp.tile(jnp.cos(ang), (1, ATTN_HEADS)), jnp.tile(jnp.sin(ang), (1, ATTN_HEADS))


def _pad_lanes(t):
    return jnp.pad(t, ((0, 0), (0, V7X_LANES - t.shape[-1])))


def kernel(x, attn_pre_norm, w_in, conv_dw, conv_dw_bias, conv_norm_gain, conv_norm_bias, gdn_short_conv, gdn_a_log,
           gdn_dt_bias, gdn_out_norm, w_out, attn_post_norm, ffn_pre_norm, w_gate, w_up, w_down, ffn_post_norm):
    B, S, D = x.shape
    depth = w_in.shape[0]
    T = B * S
    tm = 512
    w_in_packed = _pack_in_weight(w_in)
    w_out_b, w_gate_b, w_up_b, w_down_b = (t.astype(BF16) for t in (w_out, w_gate, w_up, w_down))
    cos, sin = _rotary_tables(S)
    x2d = x.reshape(T, D)
    for i in range(depth):
        *attn_parts, conv_u, gdn_qkv, gdn_gate, gdn_ab = _inproj(
            x2d, attn_pre_norm[i][None, :], cos, sin, w_in_packed[i], S, tm)
        y_attn = _attention([t.reshape(B, S, -1) for t in attn_parts])
        y_conv = _conv_mixer(conv_u.reshape(B, S, -1), conv_dw[i], conv_dw_bias[i][None, :],
                             conv_norm_gain[i][None, :], conv_norm_bias[i][None, :])
        y_gdn = _gdn_mixer(gdn_qkv.reshape(B, S, -1), gdn_gate.reshape(B, S, -1), gdn_ab.reshape(B, S, -1),
                           gdn_short_conv[i], _pad_lanes(gdn_a_log[i][None, :]), _pad_lanes(gdn_dt_bias[i][None, :]),
                           gdn_out_norm[i][None, :])
        x2d = _out_ffn(x2d, y_attn.reshape(T, -1), y_conv.reshape(T, -1), y_gdn.reshape(T, -1), w_out_b[i],
                       attn_post_norm[i][None, :], ffn_pre_norm[i][None, :], w_gate_b[i], w_up_b[i], w_down_b[i],
                       ffn_post_norm[i][None, :], tm)
    return x2d.reshape(B, S, D)
```

```python
import functools
import math

import jax
import jax.numpy as jnp
from jax import lax
from jax.experimental import pallas as pl
from jax.experimental.pallas import tpu as pltpu

F32 = jnp.float32
BF16 = jnp.bfloat16

ATTN_HEAD_DIM = 64
ATTN_HEADS = 4
ATTN_WIDTH = ATTN_HEADS * ATTN_HEAD_DIM
ATTN_BLOCK = 128
ATTN_GROUP = 8
DILATIONS = (1, 4, 16)
ROPE_THETA = 10000.0
CONV_CH = 256
CONV_WIDTH = 31
GDN_HEADS = 4
GDN_HEAD_DIM = 128
GDN_WIDTH = GDN_HEADS * GDN_HEAD_DIM
GDN_CONV_WIDTH = 4
GDN_CHUNK = 128
RMS_EPS = 1e-6
LN_EPS = 1e-5
L2_EPS = 1e-6
NEG_BIG = -1e30

V7X_LANES = 128
V7X_SUBLANES = 8
V7X_VMEM_LIMIT_BYTES = 56 * 1024 * 1024

W_ATTN = 0
W_CONV = W_ATTN + 3 * ATTN_WIDTH
W_GQKV = W_CONV + 2 * CONV_CH
W_GGATE = W_GQKV + 3 * GDN_WIDTH
W_AB = W_GGATE + GDN_WIDTH
W_TOTAL = W_AB + V7X_LANES


def _rms(x):
    return x * lax.rsqrt(jnp.mean(x * x, axis=-1, keepdims=True) + RMS_EPS)


def _sigmoid(x):
    return 1.0 / (1.0 + jnp.exp(-x))


def _silu(x):
    return x * _sigmoid(x)


def _dot(a, b, precision=None):
    return jnp.dot(a, b, preferred_element_type=F32, precision=precision)


def _dot_nt(a, b, precision=None):
    return lax.dot_general(a, b, (((1,), (1,)), ((), ())), preferred_element_type=F32, precision=precision)


def _inproj_kernel(x_ref, gain_ref, cos_ref, sin_ref, w_ref, q1_ref, q2_ref, k1_ref, k2_ref, va_ref, vb_ref,
                   conv_ref, gqkv_ref, ggate_ref, ab_ref):
    h = (_rms(x_ref[...]) * gain_ref[...]).astype(BF16)

    def proj(lo, hi):
        return _dot(h, w_ref[:, lo:hi])

    qk = proj(W_ATTN, W_ATTN + 2 * ATTN_WIDTH)
    cos = cos_ref[...]
    sin = sin_ref[...]
    half = ATTN_WIDTH // 2
    q1, q2 = qk[:, 0:half], qk[:, half:2 * half]
    k1, k2 = qk[:, 2 * half:3 * half], qk[:, 3 * half:4 * half]
    scale = ATTN_HEAD_DIM ** -0.5
    q1_ref[...] = (q1 * cos - q2 * sin) * scale
    q2_ref[...] = (q2 * cos + q1 * sin) * scale
    k1_ref[...] = k1 * cos - k2 * sin
    k2_ref[...] = k2 * cos + k1 * sin
    v = proj(W_ATTN + 2 * ATTN_WIDTH, W_CONV)
    va_ref[...] = v[:, 0:half]
    vb_ref[...] = v[:, half:2 * half]
    conv_ref[...] = proj(W_CONV, W_GQKV)
    gqkv_ref[...] = proj(W_GQKV, W_GGATE)
    ggate_ref[...] = proj(W_GGATE, W_AB)
    ab_ref[...] = proj(W_AB, W_TOTAL)


def _inproj(x2d, gain, cos, sin, w, seq, tm):
    T, D = x2d.shape
    blocks_per_seq = seq // tm
    row = lambda i: (i, 0)
    const = lambda i: (0, 0)
    widths = (ATTN_WIDTH // 2,) * 6 + (2 * CONV_CH, 3 * GDN_WIDTH, GDN_WIDTH, V7X_LANES)
    return pl.pallas_call(
        _inproj_kernel,
        name="inproj",
        grid=(T // tm,),
        in_specs=[
            pl.BlockSpec((tm, D), row),
            pl.BlockSpec((1, D), const),
            pl.BlockSpec((tm, ATTN_WIDTH // 2), lambda i: (i % blocks_per_seq, 0)),
            pl.BlockSpec((tm, ATTN_WIDTH // 2), lambda i: (i % blocks_per_seq, 0)),
            pl.BlockSpec((D, W_TOTAL), const),
        ],
        out_specs=[pl.BlockSpec((tm, n), row) for n in widths],
        out_shape=[jax.ShapeDtypeStruct((T, n), F32) for n in widths],
        compiler_params=pltpu.CompilerParams(
            dimension_semantics=("arbitrary",), vmem_limit_bytes=V7X_VMEM_LIMIT_BYTES),
    )(x2d, gain, cos, sin, w)


def _attn_units(units):
    heads = range(ATTN_HEADS)
    lane = lax.broadcasted_iota(jnp.int32, (1, ATTN_WIDTH), 1)
    qk_head = (lane % (ATTN_WIDTH // 2)) // (ATTN_HEAD_DIM // 2)
    scores = []
    for q, k, _, _ in units:
        qb = q.astype(BF16)
        zero = jnp.zeros_like(qb)
        q_stack = jnp.concatenate([jnp.where(qk_head == h, qb, zero) for h in heads], axis=0)
        scores.append(_dot_nt(k.astype(BF16), q_stack))
    stats = []
    for s, (_, _, _, valid_t) in zip(scores, units):
        s = jnp.where(jnp.concatenate([valid_t] * ATTN_HEADS, axis=1), s, NEG_BIG)
        m = jnp.max(s, axis=0, keepdims=True)
        p = jnp.exp(s - m)
        l = jnp.sum(p, axis=0, keepdims=True)
        stats.append((p.astype(BF16), 1.0 / l, m + jnp.log(l)))
    outs = [lax.dot_general(v.astype(BF16), p, (((0,), (0,)), ((), ())), preferred_element_type=F32)
            for (p, _, _), (_, _, v, _) in zip(stats, units)]
    results = []
    for o_t, (_, inv_l, lse), (q, _, _, _) in zip(outs, stats, units):
        nq = q.shape[0]
        d = ATTN_HEAD_DIM
        out_t = jnp.concatenate(
            [o_t[h * d:(h + 1) * d, h * nq:(h + 1) * nq] * inv_l[:, h * nq:(h + 1) * nq] for h in heads], axis=0)
        lse_t = jnp.concatenate(
            [jnp.broadcast_to(lse[:, h * nq:(h + 1) * nq], (d, nq)) for h in heads], axis=0)
        results.append((out_t.T, lse_t.T))
    return results


def _attn_kernel(q1_ref, q2_ref, k1_ref, k2_ref, va_ref, vb_ref, o_ref,
                 o2a_ref, o2b_ref, l2a_ref, l2b_ref, o3a_ref, o3b_ref, l3a_ref, l3b_ref):
    n = ATTN_BLOCK
    seq = q1_ref.shape[0]
    ki = lax.broadcasted_iota(jnp.int32, (2 * n, n), 0)
    qi = lax.broadcasted_iota(jnp.int32, (2 * n, n), 1)
    valid_band = (ki >= qi) & (ki <= qi + n)
    valid_first = lax.broadcasted_iota(jnp.int32, (n, n), 0) <= lax.broadcasted_iota(jnp.int32, (n, n), 1)

    def rows(start, size, stride):
        if stride == 1:
            return pl.ds(start, size)
        return pl.ds(start, size, stride=stride)

    def load(refs, sl):
        return jnp.concatenate([r[sl, :] for r in refs], axis=1)

    def store(refs, sl, val):
        for j, r in enumerate(refs):
            r[sl, :] = val[:, j * V7X_LANES:(j + 1) * V7X_LANES]

    def run_units(specs):
        units = []
        for q_start, stride, first in specs:
            q = load((q1_ref, q2_ref), rows(q_start, n, stride))
            if first:
                k_rows = rows(q_start, n, stride)
                valid = valid_first
            else:
                k_rows = rows(q_start - n * stride, 2 * n, stride)
                valid = valid_band
            units.append((q, load((k1_ref, k2_ref), k_rows), load((va_ref, vb_ref), k_rows), valid))
        return _attn_units(units)

    def run_dilated(specs, o_refs, l_refs):
        for (q_start, stride, _), (o, lse) in zip(specs, run_units(specs)):
            store(o_refs, rows(q_start, n, stride), o)
            store(l_refs, rows(q_start, n, stride), lse)

    d2, d3 = DILATIONS[1], DILATIONS[2]
    blocks2 = seq // (n * d2)
    residues2 = ATTN_GROUP // blocks2
    assert seq == n * d3 and d3 % ATTN_GROUP == 0 and ATTN_GROUP % blocks2 == 0 and d2 % residues2 == 0
    assert (seq // n) % ATTN_GROUP == 0

    def widest_body(g, carry):
        specs = [(g * ATTN_GROUP + u, d3, True) for u in range(ATTN_GROUP)]
        run_dilated(specs, (o3a_ref, o3b_ref), (l3a_ref, l3b_ref))
        return carry

    lax.fori_loop(0, d3 // ATTN_GROUP, widest_body, 0)

    def middle_body(g, carry):
        specs = [(b * n * d2 + g * residues2 + j, d2, b == 0) for j in range(residues2) for b in range(blocks2)]
        run_dilated(specs, (o2a_ref, o2b_ref), (l2a_ref, l2b_ref))
        return carry

    lax.fori_loop(0, d2 // residues2, middle_body, 0)

    def run_dense(specs):
        for (start, _, _), (o1, l1) in zip(specs, run_units(specs)):
            sl = pl.ds(start, n)
            o2, l2 = load((o2a_ref, o2b_ref), sl), load((l2a_ref, l2b_ref), sl)
            o3, l3 = load((o3a_ref, o3b_ref), sl), load((l3a_ref, l3b_ref), sl)
            top = jnp.maximum(jnp.maximum(l1, l2), l3)
            w1, w2, w3 = jnp.exp(l1 - top), jnp.exp(l2 - top), jnp.exp(l3 - top)
            o_ref[sl, :] = (w1 * o1 + w2 * o2 + w3 * o3) / (w1 + w2 + w3)

    run_dense([(u * n, 1, u == 0) for u in range(ATTN_GROUP)])

    def dense_body(g, carry):
        base = pl.multiple_of(g * (ATTN_GROUP * n), ATTN_GROUP * n)
        run_dense([(base + u * n, 1, False) for u in range(ATTN_GROUP)])
        return carry

    lax.fori_loop(1, seq // (ATTN_GROUP * n), dense_body, 0)


def _attention(parts):
    B, S, W = parts[0].shape
    blk = lambda b: (b, 0, 0)
    return pl.pallas_call(
        _attn_kernel,
        name="dilated_attn",
        grid=(B,),
        in_specs=[pl.BlockSpec((None, S, W), blk) for _ in parts],
        out_specs=pl.BlockSpec((None, S, ATTN_WIDTH), blk),
        out_shape=jax.ShapeDtypeStruct((B, S, ATTN_WIDTH), F32),
        scratch_shapes=[pltpu.VMEM((S, V7X_LANES), F32) for _ in range(8)],
        compiler_params=pltpu.CompilerParams(
            dimension_semantics=("arbitrary",), vmem_limit_bytes=V7X_VMEM_LIMIT_BYTES),
    )(*parts)


CONV_PAD = 32
CONV_ROWS = 128


def _conv_kernel(u_ref, dw_ref, bias_ref, gain_ref, nbias_ref, o_ref, g_ref):
    seq = u_ref.shape[0]
    g_ref[0:CONV_PAD, :] = jnp.zeros((CONV_PAD, CONV_CH), F32)
    g_ref[CONV_PAD:CONV_PAD + seq, :] = u_ref[:, 0:CONV_CH] * _sigmoid(u_ref[:, CONV_CH:2 * CONV_CH])
    dw = dw_ref[...]
    bias = bias_ref[...]
    gain = gain_ref[...]
    nbias = nbias_ref[...]
    lead = CONV_PAD - (CONV_WIDTH - 1)
    sub = V7X_SUBLANES

    def body(c, carry):
        r0 = pl.multiple_of(c * CONV_ROWS, CONV_ROWS)
        acc = jnp.zeros((CONV_ROWS, CONV_CH), F32) + bias
        for rem in range(sub):
            taps = [j for j in range(CONV_WIDTH) if (lead + j) % sub == rem]
            nrows = CONV_ROWS if rem == 0 else CONV_ROWS + sub
            part = None
            for j in taps:
                start = pl.multiple_of(r0 + (lead + j - rem), sub)
                term = dw[j:j + 1, :] * g_ref[pl.ds(start, nrows), :]
                part = term if part is None else part + term
            if rem:
                part = pltpu.roll(part, nrows - rem, axis=0)[0:CONV_ROWS, :]
            acc = acc + part
        mu = jnp.mean(acc, axis=-1, keepdims=True)
        cen = acc - mu
        var = jnp.mean(cen * cen, axis=-1, keepdims=True)
        y = cen * lax.rsqrt(var + LN_EPS) * gain + nbias
        o_ref[pl.ds(r0, CONV_ROWS), :] = _silu(y)
        return carry

    lax.fori_loop(0, seq // CONV_ROWS, body, 0)


def _conv_mixer(u, dw, bias, gain, nbias):
    B, S, W = u.shape
    blk = lambda b: (b, 0, 0)
    const = lambda b: (0, 0)
    return pl.pallas_call(
        _conv_kernel,
        name="conformer_conv",
        grid=(B,),
        in_specs=[
            pl.BlockSpec((None, S, W), blk),
            pl.BlockSpec(dw.shape, const),
            pl.BlockSpec((1, CONV_CH), const),
            pl.BlockSpec((1, CONV_CH), const),
            pl.BlockSpec((1, CONV_CH), const),
        ],
        out_specs=pl.BlockSpec((None, S, CONV_CH), blk),
        out_shape=jax.ShapeDtypeStruct((B, S, CONV_CH), F32),
        scratch_shapes=[pltpu.VMEM((CONV_PAD + S, CONV_CH), F32)],
        compiler_params=pltpu.CompilerParams(
            dimension_semantics=("arbitrary",), vmem_limit_bytes=V7X_VMEM_LIMIT_BYTES),
    )(u, dw, bias, gain, nbias)


GDN_SEQ_BLOCK = 256
GDN_SEQS = 2
HIGHEST = lax.Precision.HIGHEST


def _unit_lower_inverses(m_list, rows, cols):
    size = m_list[0].shape[0]
    eye = (rows == cols).astype(F32)
    link = (rows // 2 == cols // 2) & (rows == cols + 1)
    ts = [eye - jnp.where(link, m, 0.0) for m in m_list]
    neg_mbs = [(-m).astype(BF16) for m in m_list]
    step = 2
    while step < size:
        link = ((rows // step) == (cols // step) + 1) & ((rows // (2 * step)) == (cols // (2 * step)))
        tbs = [t.astype(BF16) for t in ts]
        mts = [_dot(mb, tb).astype(BF16) for mb, tb in zip(neg_mbs, tbs)]
        ts = [jnp.where(link, _dot(tb, mt), t) for t, tb, mt in zip(ts, tbs, mts)]
        step *= 2
    return ts


def _gdn_kernel(qkv_ref, gate_ref, ab_ref, cw_ref, alog_ref, dtb_ref, onorm_ref, o_ref, state_ref, halo_ref):
    C = GDN_CHUNK
    D = GDN_HEAD_DIM
    W = GDN_WIDTH
    nseq, rows_blk = qkv_ref.shape[0], qkv_ref.shape[1]
    nchunks = rows_blk // C
    halo = halo_ref.shape[1]
    last = GDN_CONV_WIDTH - 1
    heads = range(GDN_HEADS)

    @pl.when(pl.program_id(1) == 0)
    def _():
        state_ref[...] = jnp.zeros_like(state_ref)
        halo_ref[...] = jnp.zeros_like(halo_ref)

    cw = cw_ref[...]
    alog = alog_ref[...]
    dtb = dtb_ref[...]
    onorm = onorm_ref[...]
    ri = lax.broadcasted_iota(jnp.int32, (C, C), 0)
    ci = lax.broadcasted_iota(jnp.int32, (C, C), 1)
    lower_incl = ri >= ci
    strict = ri > ci
    tril_ones = lower_incl.astype(F32)

    chains = []
    for b in range(nseq):
        for c in range(nchunks):
            r0 = c * C
            prev = halo_ref[b] if c == 0 else qkv_ref[b, r0 - halo:r0, :]
            win = jnp.concatenate([prev, qkv_ref[b, r0:r0 + C, :]], axis=0)
            acc = cw[last:last + 1, :] * win[halo:, :]
            for j in range(last):
                acc = acc + cw[j:j + 1, :] * pltpu.roll(win, last - j, axis=0)[halo:, :]
            qkv = _silu(acc)

            ab = ab_ref[b, r0:r0 + C, :]
            xa = ab + dtb
            softplus = jnp.maximum(xa, 0.0) + jnp.log1p(jnp.exp(-jnp.abs(xa)))
            g = -jnp.exp(alog) * softplus
            gc = _dot(tril_ones, g, HIGHEST)
            gc_t = gc.T
            beta_all = _sigmoid(ab)

            for h in heads:
                q = qkv[:, h * D:(h + 1) * D]
                k = qkv[:, W + h * D:W + (h + 1) * D]
                v = qkv[:, 2 * W + h * D:2 * W + (h + 1) * D]
                q = q * lax.rsqrt(jnp.sum(q * q, axis=-1, keepdims=True) + L2_EPS) * (D ** -0.5)
                k = k * lax.rsqrt(jnp.sum(k * k, axis=-1, keepdims=True) + L2_EPS)
                g_col = gc[:, h:h + 1]
                g_row = gc_t[h:h + 1, :]
                beta = beta_all[:, GDN_HEADS + h:GDN_HEADS + h + 1]
                decay = jnp.exp(jnp.where(lower_incl, g_col - g_row, NEG_BIG))
                kb = k * beta
                g_last = g_col[C - 1:C, :]
                chains.append(dict(
                    decay=decay,
                    kb=kb.astype(BF16),
                    k=k.astype(BF16),
                    q=q.astype(BF16),
                    rhs=jnp.concatenate([v * beta, kb * jnp.exp(g_col)], axis=1).astype(BF16),
                    q_dec=(q * jnp.exp(g_col)).astype(BF16),
                    k_dec_t=(k * jnp.exp(g_last - g_col)).T.astype(BF16),
                    carry=jnp.exp(g_last),
                ))

    m_list = [jnp.where(strict, _dot_nt(p["kb"], p["k"]) * p["decay"], 0.0) for p in chains]
    t_list = _unit_lower_inverses(m_list, ri, ci)
    sols = [_dot(t.astype(BF16), p["rhs"]) for t, p in zip(t_list, chains)]
    intras = [(_dot_nt(p["q"], p["k"]) * p["decay"]).astype(BF16) for p in chains]

    lanes = [(b, h) for b in range(nseq) for h in heads]
    states = [state_ref[b, h] for b, h in lanes]
    for c in range(nchunks):
        r0 = c * C
        idx = [(b * nchunks + c) * GDN_HEADS + h for b, h in lanes]
        sbs = [s.astype(BF16) for s in states]
        v_news = [(sols[i][:, :D] - _dot(sols[i][:, D:].astype(BF16), sb)).astype(BF16) for i, sb in zip(idx, sbs)]
        outs = [_dot(chains[i]["q_dec"], sb) + _dot(intras[i], vn) for i, sb, vn in zip(idx, sbs, v_news)]
        states = [s * chains[i]["carry"] + _dot(chains[i]["k_dec_t"], vn) for i, s, vn in zip(idx, states, v_news)]
        for (b, h), out in zip(lanes, outs):
            y = _rms(out) * onorm * _silu(gate_ref[b, r0:r0 + C, h * D:(h + 1) * D])
            o_ref[b, r0:r0 + C, h * D:(h + 1) * D] = y
    for (b, h), s in zip(lanes, states):
        state_ref[b, h] = s
    for b in range(nseq):
        halo_ref[b] = qkv_ref[b, rows_blk - halo:rows_blk, :]


def _gdn_mixer(qkv, gate, ab, conv_w, a_log, dt_bias, out_norm):
    B, S, _ = qkv.shape
    sb = GDN_SEQ_BLOCK
    ns = GDN_SEQS
    assert B % ns == 0 and S % sb == 0
    blk = lambda b, s: (b, s, 0)
    const = lambda b, s: (0, 0)
    return pl.pallas_call(
        _gdn_kernel,
        name="gated_deltanet",
        grid=(B // ns, S // sb),
        in_specs=[
            pl.BlockSpec((ns, sb, 3 * GDN_WIDTH), blk),
            pl.BlockSpec((ns, sb, GDN_WIDTH), blk),
            pl.BlockSpec((ns, sb, V7X_LANES), blk),
            pl.BlockSpec(conv_w.shape, const),
            pl.BlockSpec((1, V7X_LANES), const),
            pl.BlockSpec((1, V7X_LANES), const),
            pl.BlockSpec((1, GDN_HEAD_DIM), const),
        ],
        out_specs=pl.BlockSpec((ns, sb, GDN_WIDTH), blk),
        out_shape=jax.ShapeDtypeStruct((B, S, GDN_WIDTH), F32),
        scratch_shapes=[
            pltpu.VMEM((ns, GDN_HEADS, GDN_HEAD_DIM, GDN_HEAD_DIM), F32),
            pltpu.VMEM((ns, V7X_SUBLANES, 3 * GDN_WIDTH), F32),
        ],
        compiler_params=pltpu.CompilerParams(
            dimension_semantics=("arbitrary", "arbitrary"), vmem_limit_bytes=V7X_VMEM_LIMIT_BYTES),
    )(qkv, gate, ab, conv_w, a_log, dt_bias, out_norm)


def _out_ffn_kernel(x_ref, attn_ref, conv_ref, gdn_ref, wo_ref, g_post_ref, g_pre_ref, wg_ref, wu_ref, wd_ref,
                    g_ffn_ref, o_ref, *, hidden_chunk):
    a0, a1, a2 = ATTN_WIDTH, ATTN_WIDTH + CONV_CH, ATTN_WIDTH + CONV_CH + GDN_WIDTH
    y = _dot(attn_ref[...].astype(BF16), wo_ref[0:a0, :])
    y = y + _dot(conv_ref[...].astype(BF16), wo_ref[a0:a1, :])
    y = y + _dot(gdn_ref[...].astype(BF16), wo_ref[a1:a2, :])
    x1 = x_ref[...] + _rms(y) * g_post_ref[...]
    h = (_rms(x1) * g_pre_ref[...]).astype(BF16)
    hidden = wg_ref.shape[1]
    f = None
    for lo in range(0, hidden, hidden_chunk):
        hi = lo + hidden_chunk
        act = (_silu(_dot(h, wg_ref[:, lo:hi])) * _dot(h, wu_ref[:, lo:hi])).astype(BF16)
        part = _dot(act, wd_ref[lo:hi, :])
        f = part if f is None else f + part
    o_ref[...] = x1 + _rms(f) * g_ffn_ref[...]


def _out_ffn(x2d, attn, conv, gdn, wo, g_post, g_pre, wg, wu, wd, g_ffn, tm):
    T, D = x2d.shape
    hidden = wg.shape[1]
    hidden_chunk = hidden // 2
    row = lambda i: (i, 0)
    const = lambda i: (0, 0)
    resident = lambda shape: pl.BlockSpec(shape, const, pipeline_mode=pl.Buffered(1))
    return pl.pallas_call(
        functools.partial(_out_ffn_kernel, hidden_chunk=hidden_chunk),
        name="outproj_ffn",
        grid=(T // tm,),
        in_specs=[
            pl.BlockSpec((tm, D), row),
            pl.BlockSpec((tm, ATTN_WIDTH), row),
            pl.BlockSpec((tm, CONV_CH), row),
            pl.BlockSpec((tm, GDN_WIDTH), row),
            resident(wo.shape),
            pl.BlockSpec((1, D), const),
            pl.BlockSpec((1, D), const),
            resident(wg.shape),
            resident(wu.shape),
            resident(wd.shape),
            pl.BlockSpec((1, D), const),
        ],
        out_specs=pl.BlockSpec((tm, D), row),
        out_shape=jax.ShapeDtypeStruct((T, D), F32),
        compiler_params=pltpu.CompilerParams(
            dimension_semantics=("arbitrary",), vmem_limit_bytes=V7X_VMEM_LIMIT_BYTES),
    )(x2d, attn, conv, gdn, wo, g_post, g_pre, wg, wu, wd, g_ffn)


def _pack_in_weight(w_in):
    depth, d_model, in_width = w_in.shape
    wb = w_in.astype(BF16)
    half = ATTN_HEAD_DIM // 2
    qk = wb[:, :, :2 * ATTN_WIDTH].reshape(depth, d_model, 2, ATTN_HEADS, 2, half)
    qk = qk.transpose(0, 1, 2, 4, 3, 5).reshape(depth, d_model, 2 * ATTN_WIDTH)
    pad = jnp.zeros((depth, d_model, W_TOTAL - in_width), BF16)
    return jnp.concatenate([qk, wb[:, :, 2 * ATTN_WIDTH:], pad], axis=-1)


def _rotary_tables(seq):
    half = ATTN_HEAD_DIM // 2
    inv_freq = jnp.exp(-math.log(ROPE_THETA) * jnp.arange(half, dtype=F32) * (2.0 / ATTN_HEAD_DIM))
    ang = jnp.arange(seq).astype(F32)[:, None] * inv_freq[None, :]
    return jnp.tile(jnp.cos(ang), (1, ATTN_HEADS)), jnp.tile(jnp.sin(ang), (1, ATTN_HEADS))


def _pad_lanes(t):
    return jnp.pad(t, ((0, 0), (0, V7X_LANES - t.shape[-1])))


def kernel(x, attn_pre_norm, w_in, conv_dw, conv_dw_bias, conv_norm_gain, conv_norm_bias, gdn_short_conv, gdn_a_log,
           gdn_dt_bias, gdn_out_norm, w_out, attn_post_norm, ffn_pre_norm, w_gate, w_up, w_down, ffn_post_norm):
    B, S, D = x.shape
    depth = w_in.shape[0]
    T = B * S
    tm = 512
    w_in_packed = _pack_in_weight(w_in)
    w_out_b, w_gate_b, w_up_b, w_down_b = (t.astype(BF16) for t in (w_out, w_gate, w_up, w_down))
    cos, sin = _rotary_tables(S)
    x2d = x.reshape(T, D)
    for i in range(depth):
        *attn_parts, conv_u, gdn_qkv, gdn_gate, gdn_ab = _inproj(
            x2d, attn_pre_norm[i][None, :], cos, sin, w_in_packed[i], S, tm)
        y_attn = _attention([t.reshape(B, S, -1) for t in attn_parts])
        y_conv = _conv_mixer(conv_u.reshape(B, S, -1), conv_dw[i], conv_dw_bias[i][None, :],
                             conv_norm_gain[i][None, :], conv_norm_bias[i][None, :])
        y_gdn = _gdn_mixer(gdn_qkv.reshape(B, S, -1), gdn_gate.reshape(B, S, -1), gdn_ab.reshape(B, S, -1),
                           gdn_short_conv[i], _pad_lanes(gdn_a_log[i][None, :]), _pad_lanes(gdn_dt_bias[i][None, :]),
                           gdn_out_norm[i][None, :])
        x2d = _out_ffn(x2d, y_attn.reshape(T, -1), y_conv.reshape(T, -1), y_gdn.reshape(T, -1), w_out_b[i],
                       attn_post_norm[i][None, :], ffn_pre_norm[i][None, :], w_gate_b[i], w_up_b[i], w_down_b[i],
                       ffn_post_norm[i][None, :], tm)
    return x2d.reshape(B, S, D)
```

```python
import math

import jax
import jax.numpy as jnp
from jax import lax
from jax.experimental import pallas as pl
from jax.experimental.pallas import tpu as pltpu

F32 = jnp.float32
BF16 = jnp.bfloat16

ATTN_HEAD_DIM = 64
ATTN_HEADS = 4
ATTN_WIDTH = ATTN_HEADS * ATTN_HEAD_DIM
ATTN_BLOCK = 128
ATTN_GROUP = 8
DILATIONS = (1, 4, 16)
ROPE_THETA = 10000.0
CONV_CH = 256
CONV_WIDTH = 31
GDN_HEADS = 4
GDN_HEAD_DIM = 128
GDN_WIDTH = GDN_HEADS * GDN_HEAD_DIM
GDN_CONV_WIDTH = 4
GDN_CHUNK = 128
RMS_EPS = 1e-6
LN_EPS = 1e-5
L2_EPS = 1e-6
NEG_BIG = -1e30

V7X_LANES = 128
V7X_SUBLANES = 8
V7X_VMEM_LIMIT_BYTES = 56 * 1024 * 1024

W_ATTN = 0
W_CONV = W_ATTN + 3 * ATTN_WIDTH
W_GQKV = W_CONV + 2 * CONV_CH
W_GGATE = W_GQKV + 3 * GDN_WIDTH
W_AB = W_GGATE + GDN_WIDTH
W_TOTAL = W_AB + V7X_LANES


def _rms(x):
    return x * lax.rsqrt(jnp.mean(x * x, axis=-1, keepdims=True) + RMS_EPS)


def _sigmoid(x):
    return 1.0 / (1.0 + jnp.exp(-x))


def _silu(x):
    return x * _sigmoid(x)


def _dot(a, b, precision=None):
    return jnp.dot(a, b, preferred_element_type=F32, precision=precision)


def _dot_nt(a, b, precision=None):
    return lax.dot_general(a, b, (((1,), (1,)), ((), ())), preferred_element_type=F32, precision=precision)


def _inproj_kernel(x_ref, gain_ref, cos_ref, sin_ref, w_ref, q1_ref, q2_ref, k1_ref, k2_ref, va_ref, vb_ref,
                   conv_ref, gqkv_ref, ggate_ref, ab_ref):
    h = (_rms(x_ref[...]) * gain_ref[...]).astype(BF16)

    def proj(lo, hi):
        return _dot(h, w_ref[:, lo:hi])

    qk = proj(W_ATTN, W_ATTN + 2 * ATTN_WIDTH)
    cos = cos_ref[...]
    sin = sin_ref[...]
    half = ATTN_WIDTH // 2
    q1, q2 = qk[:, 0:half], qk[:, half:2 * half]
    k1, k2 = qk[:, 2 * half:3 * half], qk[:, 3 * half:4 * half]
    scale = ATTN_HEAD_DIM ** -0.5
    q1_ref[...] = (q1 * cos - q2 * sin) * scale
    q2_ref[...] = (q2 * cos + q1 * sin) * scale
    k1_ref[...] = k1 * cos - k2 * sin
    k2_ref[...] = k2 * cos + k1 * sin
    v = proj(W_ATTN + 2 * ATTN_WIDTH, W_CONV)
    va_ref[...] = v[:, 0:half]
    vb_ref[...] = v[:, half:2 * half]
    conv_ref[...] = proj(W_CONV, W_GQKV)
    gqkv_ref[...] = proj(W_GQKV, W_GGATE)
    ggate_ref[...] = proj(W_GGATE, W_AB)
    ab_ref[...] = proj(W_AB, W_TOTAL)


def _inproj(x2d, gain, cos, sin, w, seq, tm):
    T, D = x2d.shape
    blocks_per_seq = seq // tm
    row = lambda i: (i, 0)
    const = lambda i: (0, 0)
    widths = (ATTN_WIDTH // 2,) * 6 + (2 * CONV_CH, 3 * GDN_WIDTH, GDN_WIDTH, V7X_LANES)
    return pl.pallas_call(
        _inproj_kernel,
        name="inproj",
        grid=(T // tm,),
        in_specs=[
            pl.BlockSpec((tm, D), row),
            pl.BlockSpec((1, D), const),
            pl.BlockSpec((tm, ATTN_WIDTH // 2), lambda i: (i % blocks_per_seq, 0)),
            pl.BlockSpec((tm, ATTN_WIDTH // 2), lambda i: (i % blocks_per_seq, 0)),
            pl.BlockSpec((D, W_TOTAL), const),
        ],
        out_specs=[pl.BlockSpec((tm, n), row) for n in widths],
        out_shape=[jax.ShapeDtypeStruct((T, n), F32) for n in widths],
        compiler_params=pltpu.CompilerParams(
            dimension_semantics=("arbitrary",), vmem_limit_bytes=V7X_VMEM_LIMIT_BYTES),
    )(x2d, gain, cos, sin, w)


def _attn_units(units):
    heads = range(ATTN_HEADS)
    lane = lax.broadcasted_iota(jnp.int32, (1, ATTN_WIDTH), 1)
    qk_head = (lane % (ATTN_WIDTH // 2)) // (ATTN_HEAD_DIM // 2)
    scores = []
    for q, k, _, _ in units:
        qb = q.astype(BF16)
        zero = jnp.zeros_like(qb)
        q_stack = jnp.concatenate([jnp.where(qk_head == h, qb, zero) for h in heads], axis=0)
        scores.append(_dot_nt(k.astype(BF16), q_stack))
    stats = []
    for s, (_, _, _, valid_t) in zip(scores, units):
        s = jnp.where(jnp.concatenate([valid_t] * ATTN_HEADS, axis=1), s, NEG_BIG)
        m = jnp.max(s, axis=0, keepdims=True)
        p = jnp.exp(s - m)
        l = jnp.sum(p, axis=0, keepdims=True)
        stats.append((p.astype(BF16), 1.0 / l, m + jnp.log(l)))
    outs = [lax.dot_general(v.astype(BF16), p, (((0,), (0,)), ((), ())), preferred_element_type=F32)
            for (p, _, _), (_, _, v, _) in zip(stats, units)]
    results = []
    for o_t, (_, inv_l, lse), (q, _, _, _) in zip(outs, stats, units):
        nq = q.shape[0]
        d = ATTN_HEAD_DIM
        out_t = jnp.concatenate(
            [o_t[h * d:(h + 1) * d, h * nq:(h + 1) * nq] * inv_l[:, h * nq:(h + 1) * nq] for h in heads], axis=0)
        lse_t = jnp.concatenate(
            [jnp.broadcast_to(lse[:, h * nq:(h + 1) * nq], (d, nq)) for h in heads], axis=0)
        results.append((out_t.T, lse_t.T))
    return results


def _attn_kernel(q1_ref, q2_ref, k1_ref, k2_ref, va_ref, vb_ref, o_ref,
                 o2a_ref, o2b_ref, l2a_ref, l2b_ref, o3a_ref, o3b_ref, l3a_ref, l3b_ref):
    n = ATTN_BLOCK
    seq = q1_ref.shape[0]
    ki = lax.broadcasted_iota(jnp.int32, (2 * n, n), 0)
    qi = lax.broadcasted_iota(jnp.int32, (2 * n, n), 1)
    valid_band = (ki >= qi) & (ki <= qi + n)
    valid_first = lax.broadcasted_iota(jnp.int32, (n, n), 0) <= lax.broadcasted_iota(jnp.int32, (n, n), 1)

    def rows(start, size, stride):
        if stride == 1:
            return pl.ds(start, size)
        return pl.ds(start, size, stride=stride)

    def load(refs, sl):
        return jnp.concatenate([r[sl, :] for r in refs], axis=1)

    def store(refs, sl, val):
        for j, r in enumerate(refs):
            r[sl, :] = val[:, j * V7X_LANES:(j + 1) * V7X_LANES]

    def run_units(specs):
        units = []
        for q_start, stride, first in specs:
            q = load((q1_ref, q2_ref), rows(q_start, n, stride))
            if first:
                k_rows = rows(q_start, n, stride)
                valid = valid_first
            else:
                k_rows = rows(q_start - n * stride, 2 * n, stride)
                valid = valid_band
            units.append((q, load((k1_ref, k2_ref), k_rows), load((va_ref, vb_ref), k_rows), valid))
        return _attn_units(units)

    def run_dilated(specs, o_refs, l_refs):
        for (q_start, stride, _), (o, lse) in zip(specs, run_units(specs)):
            store(o_refs, rows(q_start, n, stride), o)
            store(l_refs, rows(q_start, n, stride), lse)

    d2, d3 = DILATIONS[1], DILATIONS[2]
    blocks2 = seq // (n * d2)
    residues2 = ATTN_GROUP // blocks2
    assert seq == n * d3 and d3 % ATTN_GROUP == 0 and ATTN_GROUP % blocks2 == 0 and d2 % residues2 == 0
    assert (seq // n) % ATTN_GROUP == 0

    def widest_body(g, carry):
        specs = [(g * ATTN_GROUP + u, d3, True) for u in range(ATTN_GROUP)]
        run_dilated(specs, (o3a_ref, o3b_ref), (l3a_ref, l3b_ref))
        return carry

    lax.fori_loop(0, d3 // ATTN_GROUP, widest_body, 0)

    def middle_body(g, carry):
        specs = [(b * n * d2 + g * residues2 + j, d2, b == 0) for j in range(residues2) for b in range(blocks2)]
        run_dilated(specs, (o2a_ref, o2b_ref), (l2a_ref, l2b_ref))
        return carry

    lax.fori_loop(0, d2 // residues2, middle_body, 0)

    def run_dense(specs):
        for (start, _, _), (o1, l1) in zip(specs, run_units(specs)):
            sl = pl.ds(start, n)
            o2, l2 = load((o2a_ref, o2b_ref), sl), load((l2a_ref, l2b_ref), sl)
            o3, l3 = load((o3a_ref, o3b_ref), sl), load((l3a_ref, l3b_ref), sl)
            top = jnp.maximum(jnp.maximum(l1, l2), l3)
            w1, w2, w3 = jnp.exp(l1 - top), jnp.exp(l2 - top), jnp.exp(l3 - top)
            o_ref[sl, :] = (w1 * o1 + w2 * o2 + w3 * o3) / (w1 + w2 + w3)

    run_dense([(u * n, 1, u == 0) for u in range(ATTN_GROUP)])

    def dense_body(g, carry):
        base = pl.multiple_of(g * (ATTN_GROUP * n), ATTN_GROUP * n)
        run_dense([(base + u * n, 1, False) for u in range(ATTN_GROUP)])
        return carry

    lax.fori_loop(1, seq // (ATTN_GROUP * n), dense_body, 0)


def _attention(parts):
    B, S, W = parts[0].shape
    blk = lambda b: (b, 0, 0)
    return pl.pallas_call(
        _attn_kernel,
        name="dilated_attn",
        grid=(B,),
        in_specs=[pl.BlockSpec((None, S, W), blk) for _ in parts],
        out_specs=pl.BlockSpec((None, S, ATTN_WIDTH), blk),
        out_shape=jax.ShapeDtypeStruct((B, S, ATTN_WIDTH), F32),
        scratch_shapes=[pltpu.VMEM((S, V7X_LANES), F32) for _ in range(8)],
        compiler_params=pltpu.CompilerParams(
            dimension_semantics=("arbitrary",), vmem_limit_bytes=V7X_VMEM_LIMIT_BYTES),
    )(*parts)


CONV_PAD = 32
CONV_ROWS = 128


def _conv_kernel(u_ref, dw_ref, bias_ref, gain_ref, nbias_ref, o_ref, g_ref):
    seq = u_ref.shape[0]
    g_ref[0:CONV_PAD, :] = jnp.zeros((CONV_PAD, CONV_CH), F32)
    g_ref[CONV_PAD:CONV_PAD + seq, :] = u_ref[:, 0:CONV_CH] * _sigmoid(u_ref[:, CONV_CH:2 * CONV_CH])
    dw = dw_ref[...]
    bias = bias_ref[...]
    gain = gain_ref[...]
    nbias = nbias_ref[...]
    lead = CONV_PAD - (CONV_WIDTH - 1)
    sub = V7X_SUBLANES

    def body(c, carry):
        r0 = pl.multiple_of(c * CONV_ROWS, CONV_ROWS)
        acc = jnp.zeros((CONV_ROWS, CONV_CH), F32) + bias
        for rem in range(sub):
            taps = [j for j in range(CONV_WIDTH) if (lead + j) % sub == rem]
            nrows = CONV_ROWS if rem == 0 else CONV_ROWS + sub
            part = None
            for j in taps:
                start = pl.multiple_of(r0 + (lead + j - rem), sub)
                term = dw[j:j + 1, :] * g_ref[pl.ds(start, nrows), :]
                part = term if part is None else part + term
            if rem:
                part = pltpu.roll(part, nrows - rem, axis=0)[0:CONV_ROWS, :]
            acc = acc + part
        mu = jnp.mean(acc, axis=-1, keepdims=True)
        cen = acc - mu
        var = jnp.mean(cen * cen, axis=-1, keepdims=True)
        y = cen * lax.rsqrt(var + LN_EPS) * gain + nbias
        o_ref[pl.ds(r0, CONV_ROWS), :] = _silu(y)
        return carry

    lax.fori_loop(0, seq // CONV_ROWS, body, 0)


def _conv_mixer(u, dw, bias, gain, nbias):
    B, S, W = u.shape
    blk = lambda b: (b, 0, 0)
    const = lambda b: (0, 0)
    return pl.pallas_call(
        _conv_kernel,
        name="conformer_conv",
        grid=(B,),
        in_specs=[
            pl.BlockSpec((None, S, W), blk),
            pl.BlockSpec(dw.shape, const),
            pl.BlockSpec((1, CONV_CH), const),
            pl.BlockSpec((1, CONV_CH), const),
            pl.BlockSpec((1, CONV_CH), const),
        ],
        out_specs=pl.BlockSpec((None, S, CONV_CH), blk),
        out_shape=jax.ShapeDtypeStruct((B, S, CONV_CH), F32),
        scratch_shapes=[pltpu.VMEM((CONV_PAD + S, CONV_CH), F32)],
        compiler_params=pltpu.CompilerParams(
            dimension_semantics=("arbitrary",), vmem_limit_bytes=V7X_VMEM_LIMIT_BYTES),
    )(u, dw, bias, gain, nbias)


GDN_SEQ_BLOCK = 256
GDN_SEQS = 2
HIGHEST = lax.Precision.HIGHEST


def _unit_lower_inverses(m_list, rows, cols):
    size = m_list[0].shape[0]
    eye = (rows == cols).astype(F32)
    link = (rows // 2 == cols // 2) & (rows == cols + 1)
    ts = [eye - jnp.where(link, m, 0.0) for m in m_list]
    neg_mbs = [(-m).astype(BF16) for m in m_list]
    step = 2
    while step < size:
        link = ((rows // step) == (cols // step) + 1) & ((rows // (2 * step)) == (cols // (2 * step)))
        tbs = [t.astype(BF16) for t in ts]
        mts = [_dot(mb, tb).astype(BF16) for mb, tb in zip(neg_mbs, tbs)]
        ts = [jnp.where(link, _dot(tb, mt), t) for t, tb, mt in zip(ts, tbs, mts)]
        step *= 2
    return ts


def _gdn_kernel(qkv_ref, gate_ref, ab_ref, cw_ref, alog_ref, dtb_ref, onorm_ref, o_ref, state_ref, halo_ref):
    C = GDN_CHUNK
    D = GDN_HEAD_DIM
    W = GDN_WIDTH
    nseq, rows_blk = qkv_ref.shape[0], qkv_ref.shape[1]
    nchunks = rows_blk // C
    halo = halo_ref.shape[1]
    last = GDN_CONV_WIDTH - 1
    heads = range(GDN_HEADS)

    @pl.when(pl.program_id(1) == 0)
    def _():
        state_ref[...] = jnp.zeros_like(state_ref)
        halo_ref[...] = jnp.zeros_like(halo_ref)

    cw = cw_ref[...]
    alog = alog_ref[...]
    dtb = dtb_ref[...]
    onorm = onorm_ref[...]
    ri = lax.broadcasted_iota(jnp.int32, (C, C), 0)
    ci = lax.broadcasted_iota(jnp.int32, (C, C), 1)
    lower_incl = ri >= ci
    strict = ri > ci
    tril_ones = lower_incl.astype(F32)

    chains = []
    for b in range(nseq):
        for c in range(nchunks):
            r0 = c * C
            prev = halo_ref[b] if c == 0 else qkv_ref[b, r0 - halo:r0, :]
            win = jnp.concatenate([prev, qkv_ref[b, r0:r0 + C, :]], axis=0)
            acc = cw[last:last + 1, :] * win[halo:, :]
            for j in range(last):
                acc = acc + cw[j:j + 1, :] * pltpu.roll(win, last - j, axis=0)[halo:, :]
            qkv = _silu(acc)

            ab = ab_ref[b, r0:r0 + C, :]
            xa = ab + dtb
            softplus = jnp.maximum(xa, 0.0) + jnp.log1p(jnp.exp(-jnp.abs(xa)))
            g = -jnp.exp(alog) * softplus
            gc = _dot(tril_ones, g, HIGHEST)
            gc_t = gc.T
            beta_all = _sigmoid(ab)

            for h in heads:
                q = qkv[:, h * D:(h + 1) * D]
                k = qkv[:, W + h * D:W + (h + 1) * D]
                v = qkv[:, 2 * W + h * D:2 * W + (h + 1) * D]
                q = q * lax.rsqrt(jnp.sum(q * q, axis=-1, keepdims=True) + L2_EPS) * (D ** -0.5)
                k = k * lax.rsqrt(jnp.sum(k * k, axis=-1, keepdims=True) + L2_EPS)
                g_col = gc[:, h:h + 1]
                g_row = gc_t[h:h + 1, :]
                beta = beta_all[:, GDN_HEADS + h:GDN_HEADS + h + 1]
                decay = jnp.exp(jnp.where(lower_incl, g_col - g_row, NEG_BIG))
                kb = k * beta
                g_last = g_col[C - 1:C, :]
                chains.append(dict(
                    decay=decay,
                    kb=kb.astype(BF16),
                    k=k.astype(BF16),
                    q=q.astype(BF16),
                    rhs=jnp.concatenate([v * beta, kb * jnp.exp(g_col)], axis=1).astype(BF16),
                    q_dec=(q * jnp.exp(g_col)).astype(BF16),
                    k_dec_t=(k * jnp.exp(g_last - g_col)).T.astype(BF16),
                    carry=jnp.exp(g_last),
                ))

    m_list = [jnp.where(strict, _dot_nt(p["kb"], p["k"]) * p["decay"], 0.0) for p in chains]
    t_list = _unit_lower_inverses(m_list, ri, ci)
    sols = [_dot(t.astype(BF16), p["rhs"]) for t, p in zip(t_list, chains)]
    intras = [(_dot_nt(p["q"], p["k"]) * p["decay"]).astype(BF16) for p in chains]

    lanes = [(b, h) for b in range(nseq) for h in heads]
    states = [state_ref[b, h] for b, h in lanes]
    for c in range(nchunks):
        r0 = c * C
        idx = [(b * nchunks + c) * GDN_HEADS + h for b, h in lanes]
        sbs = [s.astype(BF16) for s in states]
        v_news = [(sols[i][:, :D] - _dot(sols[i][:, D:].astype(BF16), sb)).astype(BF16) for i, sb in zip(idx, sbs)]
        outs = [_dot(chains[i]["q_dec"], sb) + _dot(intras[i], vn) for i, sb, vn in zip(idx, sbs, v_news)]
        states = [s * chains[i]["carry"] + _dot(chains[i]["k_dec_t"], vn) for i, s, vn in zip(idx, states, v_news)]
        for (b, h), out in zip(lanes, outs):
            y = _rms(out) * onorm * _silu(gate_ref[b, r0:r0 + C, h * D:(h + 1) * D])
            o_ref[b, r0:r0 + C, h * D:(h + 1) * D] = y
    for (b, h), s in zip(lanes, states):
        state_ref[b, h] = s
    for b in range(nseq):
        halo_ref[b] = qkv_ref[b, rows_blk - halo:rows_blk, :]


def _gdn_mixer(qkv, gate, ab, conv_w, a_log, dt_bias, out_norm):
    B, S, _ = qkv.shape
    sb = GDN_SEQ_BLOCK
    ns = GDN_SEQS
    assert B % ns == 0 and S % sb == 0
    blk = lambda b, s: (b, s, 0)
    const = lambda b, s: (0, 0)
    return pl.pallas_call(
        _gdn_kernel,
        name="gated_deltanet",
        grid=(B // ns, S // sb),
        in_specs=[
            pl.BlockSpec((ns, sb, 3 * GDN_WIDTH), blk),
            pl.BlockSpec((ns, sb, GDN_WIDTH), blk),
            pl.BlockSpec((ns, sb, V7X_LANES), blk),
            pl.BlockSpec(conv_w.shape, const),
            pl.BlockSpec((1, V7X_LANES), const),
            pl.BlockSpec((1, V7X_LANES), const),
            pl.BlockSpec((1, GDN_HEAD_DIM), const),
        ],
        out_specs=pl.BlockSpec((ns, sb, GDN_WIDTH), blk),
        out_shape=jax.ShapeDtypeStruct((B, S, GDN_WIDTH), F32),
        scratch_shapes=[
            pltpu.VMEM((ns, GDN_HEADS, GDN_HEAD_DIM, GDN_HEAD_DIM), F32),
            pltpu.VMEM((ns, V7X_SUBLANES, 3 * GDN_WIDTH), F32),
        ],
        compiler_params=pltpu.CompilerParams(
            dimension_semantics=("arbitrary", "arbitrary"), vmem_limit_bytes=V7X_VMEM_LIMIT_BYTES),
    )(qkv, gate, ab, conv_w, a_log, dt_bias, out_norm)


def _out_ffn_kernel(x_ref, attn_ref, conv_ref, gdn_ref, wo_ref, g_post_ref, g_pre_ref, wg_ref, wu_ref, wd_ref,
                    g_ffn_ref, o_ref):
    a0, a1, a2 = ATTN_WIDTH, ATTN_WIDTH + CONV_CH, ATTN_WIDTH + CONV_CH + GDN_WIDTH
    tm = x_ref.shape[0]
    halves = [(0, tm // 2), (tm // 2, tm)]
    ys = []
    for lo, hi in halves:
        y = _dot(attn_ref[lo:hi, :].astype(BF16), wo_ref[0:a0, :])
        y = y + _dot(conv_ref[lo:hi, :].astype(BF16), wo_ref[a0:a1, :])
        y = y + _dot(gdn_ref[lo:hi, :].astype(BF16), wo_ref[a1:a2, :])
        ys.append(y)
    x1s = [x_ref[lo:hi, :] + _rms(y) * g_post_ref[...] for (lo, hi), y in zip(halves, ys)]
    hs = [(_rms(x1) * g_pre_ref[...]).astype(BF16) for x1 in x1s]
    acts = [(_silu(_dot(h, wg_ref[...])) * _dot(h, wu_ref[...])).astype(BF16) for h in hs]
    fs = [_dot(act, wd_ref[...]) for act in acts]
    for (lo, hi), x1, f in zip(halves, x1s, fs):
        o_ref[lo:hi, :] = x1 + _rms(f) * g_ffn_ref[...]


def _out_ffn(x2d, attn, conv, gdn, wo, g_post, g_pre, wg, wu, wd, g_ffn, tm):
    T, D = x2d.shape
    row = lambda i: (i, 0)
    const = lambda i: (0, 0)
    resident = lambda shape: pl.BlockSpec(shape, const, pipeline_mode=pl.Buffered(1))
    return pl.pallas_call(
        _out_ffn_kernel,
        name="outproj_ffn",
        grid=(T // tm,),
        in_specs=[
            pl.BlockSpec((tm, D), row),
            pl.BlockSpec((tm, ATTN_WIDTH), row),
            pl.BlockSpec((tm, CONV_CH), row),
            pl.BlockSpec((tm, GDN_WIDTH), row),
            resident(wo.shape),
            pl.BlockSpec((1, D), const),
            pl.BlockSpec((1, D), const),
            resident(wg.shape),
            resident(wu.shape),
            resident(wd.shape),
            pl.BlockSpec((1, D), const),
        ],
        out_specs=pl.BlockSpec((tm, D), row),
        out_shape=jax.ShapeDtypeStruct((T, D), F32),
        compiler_params=pltpu.CompilerParams(
            dimension_semantics=("arbitrary",), vmem_limit_bytes=V7X_VMEM_LIMIT_BYTES),
    )(x2d, attn, conv, gdn, wo, g_post, g_pre, wg, wu, wd, g_ffn)


def _pack_in_weight(w_in):
    depth, d_model, in_width = w_in.shape
    wb = w_in.astype(BF16)
    half = ATTN_HEAD_DIM // 2
    qk = wb[:, :, :2 * ATTN_WIDTH].reshape(depth, d_model, 2, ATTN_HEADS, 2, half)
    qk = qk.transpose(0, 1, 2, 4, 3, 5).reshape(depth, d_model, 2 * ATTN_WIDTH)
    pad = jnp.zeros((depth, d_model, W_TOTAL - in_width), BF16)
    return jnp.concatenate([qk, wb[:, :, 2 * ATTN_WIDTH:], pad], axis=-1)


def _rotary_tables(seq):
    half = ATTN_HEAD_DIM // 2
    inv_freq = jnp.exp(-math.log(ROPE_THETA) * jnp.arange(half, dtype=F32) * (2.0 / ATTN_HEAD_DIM))
    ang = jnp.arange(seq).astype(F32)[:, None] * inv_freq[None, :]
    return jnp.tile(jnp.cos(ang), (1, ATTN_HEADS)), jnp.tile(jnp.sin(ang), (1, ATTN_HEADS))


def _pad_lanes(t):
    return jnp.pad(t, ((0, 0), (0, V7X_LANES - t.shape[-1])))


def kernel(x, attn_pre_norm, w_in, conv_dw, conv_dw_bias, conv_norm_gain, conv_norm_bias, gdn_short_conv, gdn_a_log,
           gdn_dt_bias, gdn_out_norm, w_out, attn_post_norm, ffn_pre_norm, w_gate, w_up, w_down, ffn_post_norm):
    B, S, D = x.shape
    depth = w_in.shape[0]
    T = B * S
    tm = 512
    w_in_packed = _pack_in_weight(w_in)
    w_out_b, w_gate_b, w_up_b, w_down_b = (t.astype(BF16) for t in (w_out, w_gate, w_up, w_down))
    cos, sin = _rotary_tables(S)
    x2d = x.reshape(T, D)
    for i in range(depth):
        *attn_parts, conv_u, gdn_qkv, gdn_gate, gdn_ab = _inproj(
            x2d, attn_pre_norm[i][None, :], cos, sin, w_in_packed[i], S, tm)
        y_attn = _attention([t.reshape(B, S, -1) for t in attn_parts])
        y_conv = _conv_mixer(conv_u.reshape(B, S, -1), conv_dw[i], conv_dw_bias[i][None, :],
                             conv_norm_gain[i][None, :], conv_norm_bias[i][None, :])
        y_gdn = _gdn_mixer(gdn_qkv.reshape(B, S, -1), gdn_gate.reshape(B, S, -1), gdn_ab.reshape(B, S, -1),
                           gdn_short_conv[i], _pad_lanes(gdn_a_log[i][None, :]), _pad_lanes(gdn_dt_bias[i][None, :]),
                           gdn_out_norm[i][None, :])
        x2d = _out_ffn(x2d, y_attn.reshape(T, -1), y_conv.reshape(T, -1), y_gdn.reshape(T, -1), w_out_b[i],
                       attn_post_norm[i][None, :], ffn_pre_norm[i][None, :], w_gate_b[i], w_up_b[i], w_down_b[i],
                       ffn_post_norm[i][None, :], tm)
    return x2d.reshape(B, S, D)
```

```python
import math

import jax
import jax.numpy as jnp
from jax import lax
from jax.experimental import pallas as pl
from jax.experimental.pallas import tpu as pltpu

F32 = jnp.float32
BF16 = jnp.bfloat16

ATTN_HEAD_DIM = 64
ATTN_HEADS = 4
ATTN_WIDTH = ATTN_HEADS * ATTN_HEAD_DIM
ATTN_BLOCK = 128
ATTN_GROUP = 8
DILATIONS = (1, 4, 16)
ROPE_THETA = 10000.0
CONV_CH = 256
CONV_WIDTH = 31
GDN_HEADS = 4
GDN_HEAD_DIM = 128
GDN_WIDTH = GDN_HEADS * GDN_HEAD_DIM
GDN_CONV_WIDTH = 4
GDN_CHUNK = 128
RMS_EPS = 1e-6
LN_EPS = 1e-5
L2_EPS = 1e-6
NEG_BIG = -1e30

V7X_LANES = 128
V7X_SUBLANES = 8
V7X_VMEM_LIMIT_BYTES = 56 * 1024 * 1024

W_ATTN = 0
W_CONV = W_ATTN + 3 * ATTN_WIDTH
W_GQKV = W_CONV + 2 * CONV_CH
W_GGATE = W_GQKV + 3 * GDN_WIDTH
W_AB = W_GGATE + GDN_WIDTH
W_TOTAL = W_AB + V7X_LANES


def _rms(x):
    return x * lax.rsqrt(jnp.mean(x * x, axis=-1, keepdims=True) + RMS_EPS)


def _sigmoid(x):
    return 1.0 / (1.0 + jnp.exp(-x))


def _silu(x):
    return x * _sigmoid(x)


def _dot(a, b, precision=None):
    return jnp.dot(a, b, preferred_element_type=F32, precision=precision)


def _dot_nt(a, b, precision=None):
    return lax.dot_general(a, b, (((1,), (1,)), ((), ())), preferred_element_type=F32, precision=precision)


def _inproj_kernel(x_ref, gain_ref, cos_ref, sin_ref, w_ref, q1_ref, q2_ref, k1_ref, k2_ref, va_ref, vb_ref,
                   conv_ref, gqkv_ref, ggate_ref, ab_ref):
    h = (_rms(x_ref[...]) * gain_ref[...]).astype(BF16)

    def proj(lo, hi):
        return _dot(h, w_ref[:, lo:hi])

    qk = proj(W_ATTN, W_ATTN + 2 * ATTN_WIDTH)
    cos = cos_ref[...]
    sin = sin_ref[...]
    half = ATTN_WIDTH // 2
    q1, q2 = qk[:, 0:half], qk[:, half:2 * half]
    k1, k2 = qk[:, 2 * half:3 * half], qk[:, 3 * half:4 * half]
    scale = ATTN_HEAD_DIM ** -0.5
    q1_ref[...] = (q1 * cos - q2 * sin) * scale
    q2_ref[...] = (q2 * cos + q1 * sin) * scale
    k1_ref[...] = k1 * cos - k2 * sin
    k2_ref[...] = k2 * cos + k1 * sin
    v = proj(W_ATTN + 2 * ATTN_WIDTH, W_CONV)
    va_ref[...] = v[:, 0:half]
    vb_ref[...] = v[:, half:2 * half]
    conv_ref[...] = proj(W_CONV, W_GQKV)
    gqkv_ref[...] = proj(W_GQKV, W_GGATE)
    ggate_ref[...] = proj(W_GGATE, W_AB)
    ab_ref[...] = proj(W_AB, W_TOTAL)


def _layer_spec(stacked, layer, **kwargs):
    _, rows, cols = stacked.shape
    return pl.BlockSpec((None, rows, cols), lambda *_: (layer, 0, 0), **kwargs)


def _inproj(x2d, gain, cos, sin, w, layer, seq, tm):
    T, D = x2d.shape
    blocks_per_seq = seq // tm
    row = lambda i: (i, 0)
    widths = (ATTN_WIDTH // 2,) * 6 + (2 * CONV_CH, 3 * GDN_WIDTH, GDN_WIDTH, V7X_LANES)
    return pl.pallas_call(
        _inproj_kernel,
        name="inproj",
        grid=(T // tm,),
        in_specs=[
            pl.BlockSpec((tm, D), row),
            _layer_spec(gain, layer),
            pl.BlockSpec((tm, ATTN_WIDTH // 2), lambda i: (i % blocks_per_seq, 0)),
            pl.BlockSpec((tm, ATTN_WIDTH // 2), lambda i: (i % blocks_per_seq, 0)),
            _layer_spec(w, layer),
        ],
        out_specs=[pl.BlockSpec((tm, n), row) for n in widths],
        out_shape=[jax.ShapeDtypeStruct((T, n), F32) for n in widths],
        compiler_params=pltpu.CompilerParams(
            dimension_semantics=("arbitrary",), vmem_limit_bytes=V7X_VMEM_LIMIT_BYTES),
    )(x2d, gain, cos, sin, w)


def _attn_units(units):
    heads = range(ATTN_HEADS)
    lane = lax.broadcasted_iota(jnp.int32, (1, ATTN_WIDTH), 1)
    qk_head = (lane % (ATTN_WIDTH // 2)) // (ATTN_HEAD_DIM // 2)
    scores = []
    for q, k, _, _ in units:
        qb = q.astype(BF16)
        zero = jnp.zeros_like(qb)
        q_stack = jnp.concatenate([jnp.where(qk_head == h, qb, zero) for h in heads], axis=0)
        scores.append(_dot_nt(k.astype(BF16), q_stack))
    stats = []
    for s, (_, _, _, valid_t) in zip(scores, units):
        s = jnp.where(jnp.concatenate([valid_t] * ATTN_HEADS, axis=1), s, NEG_BIG)
        m = jnp.max(s, axis=0, keepdims=True)
        p = jnp.exp(s - m)
        l = jnp.sum(p, axis=0, keepdims=True)
        stats.append((p.astype(BF16), 1.0 / l, m + jnp.log(l)))
    outs = [lax.dot_general(v.astype(BF16), p, (((0,), (0,)), ((), ())), preferred_element_type=F32)
            for (p, _, _), (_, _, v, _) in zip(stats, units)]
    results = []
    for o_t, (_, inv_l, lse), (q, _, _, _) in zip(outs, stats, units):
        nq = q.shape[0]
        d = ATTN_HEAD_DIM
        out_t = jnp.concatenate(
            [o_t[h * d:(h + 1) * d, h * nq:(h + 1) * nq] * inv_l[:, h * nq:(h + 1) * nq] for h in heads], axis=0)
        lse_t = jnp.concatenate(
            [jnp.broadcast_to(lse[:, h * nq:(h + 1) * nq], (d, nq)) for h in heads], axis=0)
        results.append((out_t.T, lse_t.T))
    return results


def _attn_kernel(q1_ref, q2_ref, k1_ref, k2_ref, va_ref, vb_ref, o_ref,
                 o2a_ref, o2b_ref, l2a_ref, l2b_ref, o3a_ref, o3b_ref, l3a_ref, l3b_ref):
    n = ATTN_BLOCK
    seq = q1_ref.shape[0]
    ki = lax.broadcasted_iota(jnp.int32, (2 * n, n), 0)
    qi = lax.broadcasted_iota(jnp.int32, (2 * n, n), 1)
    valid_band = (ki >= qi) & (ki <= qi + n)
    valid_first = lax.broadcasted_iota(jnp.int32, (n, n), 0) <= lax.broadcasted_iota(jnp.int32, (n, n), 1)

    def rows(start, size, stride):
        if stride == 1:
            return pl.ds(start, size)
        return pl.ds(start, size, stride=stride)

    def load(refs, sl):
        return jnp.concatenate([r[sl, :] for r in refs], axis=1)

    def store(refs, sl, val):
        for j, r in enumerate(refs):
            r[sl, :] = val[:, j * V7X_LANES:(j + 1) * V7X_LANES]

    def run_units(specs):
        units = []
        for q_start, stride, first in specs:
            q = load((q1_ref, q2_ref), rows(q_start, n, stride))
            if first:
                k_rows = rows(q_start, n, stride)
                valid = valid_first
            else:
                k_rows = rows(q_start - n * stride, 2 * n, stride)
                valid = valid_band
            units.append((q, load((k1_ref, k2_ref), k_rows), load((va_ref, vb_ref), k_rows), valid))
        return _attn_units(units)

    def run_dilated(specs, o_refs, l_refs):
        for (q_start, stride, _), (o, lse) in zip(specs, run_units(specs)):
            store(o_refs, rows(q_start, n, stride), o)
            store(l_refs, rows(q_start, n, stride), lse)

    d2, d3 = DILATIONS[1], DILATIONS[2]
    blocks2 = seq // (n * d2)
    residues2 = ATTN_GROUP // blocks2
    assert seq == n * d3 and d3 % ATTN_GROUP == 0 and ATTN_GROUP % blocks2 == 0 and d2 % residues2 == 0
    assert (seq // n) % ATTN_GROUP == 0

    def widest_body(g, carry):
        specs = [(g * ATTN_GROUP + u, d3, True) for u in range(ATTN_GROUP)]
        run_dilated(specs, (o3a_ref, o3b_ref), (l3a_ref, l3b_ref))
        return carry

    lax.fori_loop(0, d3 // ATTN_GROUP, widest_body, 0)

    def middle_body(g, carry):
        specs = [(b * n * d2 + g * residues2 + j, d2, b == 0) for j in range(residues2) for b in range(blocks2)]
        run_dilated(specs, (o2a_ref, o2b_ref), (l2a_ref, l2b_ref))
        return carry

    lax.fori_loop(0, d2 // residues2, middle_body, 0)

    def run_dense(specs):
        for (start, _, _), (o1, l1) in zip(specs, run_units(specs)):
            sl = pl.ds(start, n)
            o2, l2 = load((o2a_ref, o2b_ref), sl), load((l2a_ref, l2b_ref), sl)
            o3, l3 = load((o3a_ref, o3b_ref), sl), load((l3a_ref, l3b_ref), sl)
            top = jnp.maximum(jnp.maximum(l1, l2), l3)
            w1, w2, w3 = jnp.exp(l1 - top), jnp.exp(l2 - top), jnp.exp(l3 - top)
            o_ref[sl, :] = (w1 * o1 + w2 * o2 + w3 * o3) / (w1 + w2 + w3)

    run_dense([(u * n, 1, u == 0) for u in range(ATTN_GROUP)])

    def dense_body(g, carry):
        base = pl.multiple_of(g * (ATTN_GROUP * n), ATTN_GROUP * n)
        run_dense([(base + u * n, 1, False) for u in range(ATTN_GROUP)])
        return carry

    lax.fori_loop(1, seq // (ATTN_GROUP * n), dense_body, 0)


def _attention(parts):
    B, S, W = parts[0].shape
    blk = lambda b: (b, 0, 0)
    return pl.pallas_call(
        _attn_kernel,
        name="dilated_attn",
        grid=(B,),
        in_specs=[pl.BlockSpec((None, S, W), blk) for _ in parts],
        out_specs=pl.BlockSpec((None, S, ATTN_WIDTH), blk),
        out_shape=jax.ShapeDtypeStruct((B, S, ATTN_WIDTH), F32),
        scratch_shapes=[pltpu.VMEM((S, V7X_LANES), F32) for _ in range(8)],
        compiler_params=pltpu.CompilerParams(
            dimension_semantics=("arbitrary",), vmem_limit_bytes=V7X_VMEM_LIMIT_BYTES),
    )(*parts)


CONV_PAD = 32
CONV_ROWS = 512


def _conv_kernel(u_ref, dw_ref, bias_ref, gain_ref, nbias_ref, o_ref, g_ref):
    seq = u_ref.shape[0]
    g_ref[0:CONV_PAD, :] = jnp.zeros((CONV_PAD, CONV_CH), F32)
    g_ref[CONV_PAD:CONV_PAD + seq, :] = u_ref[:, 0:CONV_CH] * _sigmoid(u_ref[:, CONV_CH:2 * CONV_CH])
    dw = dw_ref[...]
    bias = bias_ref[...]
    gain = gain_ref[...]
    nbias = nbias_ref[...]
    lead = CONV_PAD - (CONV_WIDTH - 1)
    sub = V7X_SUBLANES

    def body(c, carry):
        r0 = pl.multiple_of(c * CONV_ROWS, CONV_ROWS)
        acc = jnp.zeros((CONV_ROWS, CONV_CH), F32) + bias
        for rem in range(sub):
            taps = [j for j in range(CONV_WIDTH) if (lead + j) % sub == rem]
            nrows = CONV_ROWS if rem == 0 else CONV_ROWS + sub
            part = None
            for j in taps:
                start = pl.multiple_of(r0 + (lead + j - rem), sub)
                term = dw[j:j + 1, :] * g_ref[pl.ds(start, nrows), :]
                part = term if part is None else part + term
            if rem:
                part = pltpu.roll(part, nrows - rem, axis=0)[0:CONV_ROWS, :]
            acc = acc + part
        mu = jnp.mean(acc, axis=-1, keepdims=True)
        cen = acc - mu
        var = jnp.mean(cen * cen, axis=-1, keepdims=True)
        y = cen * lax.rsqrt(var + LN_EPS) * gain + nbias
        o_ref[pl.ds(r0, CONV_ROWS), :] = _silu(y)
        return carry

    lax.fori_loop(0, seq // CONV_ROWS, body, 0)


def _conv_mixer(u, dw, bias, gain, nbias, layer):
    B, S, W = u.shape
    blk = lambda b: (b, 0, 0)
    return pl.pallas_call(
        _conv_kernel,
        name="conformer_conv",
        grid=(B,),
        in_specs=[pl.BlockSpec((None, S, W), blk)] + [_layer_spec(t, layer) for t in (dw, bias, gain, nbias)],
        out_specs=pl.BlockSpec((None, S, CONV_CH), blk),
        out_shape=jax.ShapeDtypeStruct((B, S, CONV_CH), F32),
        scratch_shapes=[pltpu.VMEM((CONV_PAD + S, CONV_CH), F32)],
        compiler_params=pltpu.CompilerParams(
            dimension_semantics=("arbitrary",), vmem_limit_bytes=V7X_VMEM_LIMIT_BYTES),
    )(u, dw, bias, gain, nbias)


GDN_SEQ_BLOCK = 256
GDN_SEQS = 2
HIGHEST = lax.Precision.HIGHEST


def _unit_lower_inverses(m_list, rows, cols):
    size = m_list[0].shape[0]
    eye = (rows == cols).astype(F32)
    link = (rows // 2 == cols // 2) & (rows == cols + 1)
    ts = [eye - jnp.where(link, m, 0.0) for m in m_list]
    neg_mbs = [(-m).astype(BF16) for m in m_list]
    step = 2
    while step < size:
        link = ((rows // step) == (cols // step) + 1) & ((rows // (2 * step)) == (cols // (2 * step)))
        tbs = [t.astype(BF16) for t in ts]
        mts = [_dot(mb, tb).astype(BF16) for mb, tb in zip(neg_mbs, tbs)]
        ts = [jnp.where(link, _dot(tb, mt), t) for t, tb, mt in zip(ts, tbs, mts)]
        step *= 2
    return ts


def _gdn_kernel(qkv_ref, gate_ref, ab_ref, cw_ref, alog_ref, dtb_ref, onorm_ref, o_ref, state_ref, halo_ref):
    C = GDN_CHUNK
    D = GDN_HEAD_DIM
    W = GDN_WIDTH
    nseq, rows_blk = qkv_ref.shape[0], qkv_ref.shape[1]
    nchunks = rows_blk // C
    halo = halo_ref.shape[1]
    last = GDN_CONV_WIDTH - 1
    heads = range(GDN_HEADS)

    @pl.when(pl.program_id(1) == 0)
    def _():
        state_ref[...] = jnp.zeros_like(state_ref)
        halo_ref[...] = jnp.zeros_like(halo_ref)

    cw = cw_ref[...]
    alog = alog_ref[...]
    dtb = dtb_ref[...]
    onorm = onorm_ref[...]
    ri = lax.broadcasted_iota(jnp.int32, (C, C), 0)
    ci = lax.broadcasted_iota(jnp.int32, (C, C), 1)
    lower_incl = ri >= ci
    strict = ri > ci
    tril_ones = lower_incl.astype(F32)

    chains = []
    for b in range(nseq):
        for c in range(nchunks):
            r0 = c * C
            prev = halo_ref[b] if c == 0 else qkv_ref[b, r0 - halo:r0, :]
            win = jnp.concatenate([prev, qkv_ref[b, r0:r0 + C, :]], axis=0)
            acc = cw[last:last + 1, :] * win[halo:, :]
            for j in range(last):
                acc = acc + cw[j:j + 1, :] * pltpu.roll(win, last - j, axis=0)[halo:, :]
            qkv = _silu(acc)

            ab = ab_ref[b, r0:r0 + C, :]
            xa = ab + dtb
            softplus = jnp.maximum(xa, 0.0) + jnp.log1p(jnp.exp(-jnp.abs(xa)))
            g = -jnp.exp(alog) * softplus
            gc = _dot(tril_ones, g, HIGHEST)
            gc_t = gc.T
            beta_all = _sigmoid(ab)

            for h in heads:
                q = qkv[:, h * D:(h + 1) * D]
                k = qkv[:, W + h * D:W + (h + 1) * D]
                v = qkv[:, 2 * W + h * D:2 * W + (h + 1) * D]
                q = q * lax.rsqrt(jnp.sum(q * q, axis=-1, keepdims=True) + L2_EPS) * (D ** -0.5)
                k = k * lax.rsqrt(jnp.sum(k * k, axis=-1, keepdims=True) + L2_EPS)
                g_col = gc[:, h:h + 1]
                g_row = gc_t[h:h + 1, :]
                beta = beta_all[:, GDN_HEADS + h:GDN_HEADS + h + 1]
                decay = jnp.exp(jnp.where(lower_incl, g_col - g_row, NEG_BIG))
                kb = k * beta
                g_last = g_col[C - 1:C, :]
                chains.append(dict(
                    decay=decay,
                    kb=kb.astype(BF16),
                    k=k.astype(BF16),
                    q=q.astype(BF16),
                    rhs=jnp.concatenate([v * beta, kb * jnp.exp(g_col)], axis=1).astype(BF16),
                    q_dec=(q * jnp.exp(g_col)).astype(BF16),
                    k_dec_t=(k * jnp.exp(g_last - g_col)).T.astype(BF16),
                    carry=jnp.exp(g_last),
                ))

    m_list = [jnp.where(strict, _dot_nt(p["kb"], p["k"]) * p["decay"], 0.0) for p in chains]
    t_list = _unit_lower_inverses(m_list, ri, ci)
    sols = [_dot(t.astype(BF16), p["rhs"]) for t, p in zip(t_list, chains)]
    intras = [(_dot_nt(p["q"], p["k"]) * p["decay"]).astype(BF16) for p in chains]

    lanes = [(b, h) for b in range(nseq) for h in heads]
    states = [state_ref[b, h] for b, h in lanes]
    for c in range(nchunks):
        r0 = c * C
        idx = [(b * nchunks + c) * GDN_HEADS + h for b, h in lanes]
        sbs = [s.astype(BF16) for s in states]
        v_news = [(sols[i][:, :D] - _dot(sols[i][:, D:].astype(BF16), sb)).astype(BF16) for i, sb in zip(idx, sbs)]
        outs = [_dot(chains[i]["q_dec"], sb) + _dot(intras[i], vn) for i, sb, vn in zip(idx, sbs, v_news)]
        states = [s * chains[i]["carry"] + _dot(chains[i]["k_dec_t"], vn) for i, s, vn in zip(idx, states, v_news)]
        for (b, h), out in zip(lanes, outs):
            y = _rms(out) * onorm * _silu(gate_ref[b, r0:r0 + C, h * D:(h + 1) * D])
            o_ref[b, r0:r0 + C, h * D:(h + 1) * D] = y
    for (b, h), s in zip(lanes, states):
        state_ref[b, h] = s
    for b in range(nseq):
        halo_ref[b] = qkv_ref[b, rows_blk - halo:rows_blk, :]


def _gdn_mixer(qkv, gate, ab, conv_w, a_log, dt_bias, out_norm, layer):
    B, S, _ = qkv.shape
    sb = GDN_SEQ_BLOCK
    ns = GDN_SEQS
    assert B % ns == 0 and S % sb == 0
    blk = lambda b, s: (b, s, 0)
    return pl.pallas_call(
        _gdn_kernel,
        name="gated_deltanet",
        grid=(B // ns, S // sb),
        in_specs=[
            pl.BlockSpec((ns, sb, 3 * GDN_WIDTH), blk),
            pl.BlockSpec((ns, sb, GDN_WIDTH), blk),
            pl.BlockSpec((ns, sb, V7X_LANES), blk),
        ] + [_layer_spec(t, layer) for t in (conv_w, a_log, dt_bias, out_norm)],
        out_specs=pl.BlockSpec((ns, sb, GDN_WIDTH), blk),
        out_shape=jax.ShapeDtypeStruct((B, S, GDN_WIDTH), F32),
        scratch_shapes=[
            pltpu.VMEM((ns, GDN_HEADS, GDN_HEAD_DIM, GDN_HEAD_DIM), F32),
            pltpu.VMEM((ns, V7X_SUBLANES, 3 * GDN_WIDTH), F32),
        ],
        compiler_params=pltpu.CompilerParams(
            dimension_semantics=("arbitrary", "arbitrary"), vmem_limit_bytes=V7X_VMEM_LIMIT_BYTES),
    )(qkv, gate, ab, conv_w, a_log, dt_bias, out_norm)


def _out_ffn_kernel(x_ref, attn_ref, conv_ref, gdn_ref, wo_ref, g_post_ref, g_pre_ref, wg_ref, wu_ref, wd_ref,
                    g_ffn_ref, o_ref):
    a0, a1, a2 = ATTN_WIDTH, ATTN_WIDTH + CONV_CH, ATTN_WIDTH + CONV_CH + GDN_WIDTH
    tm = x_ref.shape[0]
    halves = [(0, tm // 2), (tm // 2, tm)]
    ys = []
    for lo, hi in halves:
        y = _dot(attn_ref[lo:hi, :].astype(BF16), wo_ref[0:a0, :])
        y = y + _dot(conv_ref[lo:hi, :].astype(BF16), wo_ref[a0:a1, :])
        y = y + _dot(gdn_ref[lo:hi, :].astype(BF16), wo_ref[a1:a2, :])
        ys.append(y)
    x1s = [x_ref[lo:hi, :] + _rms(y) * g_post_ref[...] for (lo, hi), y in zip(halves, ys)]
    hs = [(_rms(x1) * g_pre_ref[...]).astype(BF16) for x1 in x1s]
    acts = [(_silu(_dot(h, wg_ref[...])) * _dot(h, wu_ref[...])).astype(BF16) for h in hs]
    fs = [_dot(act, wd_ref[...]) for act in acts]
    for (lo, hi), x1, f in zip(halves, x1s, fs):
        o_ref[lo:hi, :] = x1 + _rms(f) * g_ffn_ref[...]


def _out_ffn(x2d, attn, conv, gdn, wo, g_post, g_pre, wg, wu, wd, g_ffn, layer, tm):
    T, D = x2d.shape
    row = lambda i: (i, 0)
    resident = lambda t: _layer_spec(t, layer, pipeline_mode=pl.Buffered(1))
    return pl.pallas_call(
        _out_ffn_kernel,
        name="outproj_ffn",
        grid=(T // tm,),
        in_specs=[
            pl.BlockSpec((tm, D), row),
            pl.BlockSpec((tm, ATTN_WIDTH), row),
            pl.BlockSpec((tm, CONV_CH), row),
            pl.BlockSpec((tm, GDN_WIDTH), row),
            resident(wo),
            _layer_spec(g_post, layer),
            _layer_spec(g_pre, layer),
            resident(wg),
            resident(wu),
            resident(wd),
            _layer_spec(g_ffn, layer),
        ],
        out_specs=pl.BlockSpec((tm, D), row),
        out_shape=jax.ShapeDtypeStruct((T, D), F32),
        compiler_params=pltpu.CompilerParams(
            dimension_semantics=("arbitrary",), vmem_limit_bytes=V7X_VMEM_LIMIT_BYTES),
    )(x2d, attn, conv, gdn, wo, g_post, g_pre, wg, wu, wd, g_ffn)


def _pack_in_weight(w_in):
    depth, d_model, in_width = w_in.shape
    wb = w_in.astype(BF16)
    half = ATTN_HEAD_DIM // 2
    qk = wb[:, :, :2 * ATTN_WIDTH].reshape(depth, d_model, 2, ATTN_HEADS, 2, half)
    qk = qk.transpose(0, 1, 2, 4, 3, 5).reshape(depth, d_model, 2 * ATTN_WIDTH)
    pad = jnp.zeros((depth, d_model, W_TOTAL - in_width), BF16)
    return jnp.concatenate([qk, wb[:, :, 2 * ATTN_WIDTH:], pad], axis=-1)


def _rotary_tables(seq):
    half = ATTN_HEAD_DIM // 2
    inv_freq = jnp.exp(-math.log(ROPE_THETA) * jnp.arange(half, dtype=F32) * (2.0 / ATTN_HEAD_DIM))
    ang = jnp.arange(seq).astype(F32)[:, None] * inv_freq[None, :]
    return jnp.tile(jnp.cos(ang), (1, ATTN_HEADS)), jnp.tile(jnp.sin(ang), (1, ATTN_HEADS))


def _rows(t, lanes=None):
    if lanes is not None:
        t = jnp.pad(t, ((0, 0), (0, lanes - t.shape[-1])))
    return t[:, None, :]


def kernel(x, attn_pre_norm, w_in, conv_dw, conv_dw_bias, conv_norm_gain, conv_norm_bias, gdn_short_conv, gdn_a_log,
           gdn_dt_bias, gdn_out_norm, w_out, attn_post_norm, ffn_pre_norm, w_gate, w_up, w_down, ffn_post_norm):
    B, S, D = x.shape
    depth = w_in.shape[0]
    T = B * S
    tm = 512
    w_in_packed = _pack_in_weight(w_in)
    w_out_b, w_gate_b, w_up_b, w_down_b = (t.astype(BF16) for t in (w_out, w_gate, w_up, w_down))
    cos, sin = _rotary_tables(S)
    pre_gain, post_gain, ffn_pre_gain, ffn_post_gain = (
        _rows(t) for t in (attn_pre_norm, attn_post_norm, ffn_pre_norm, ffn_post_norm))
    conv_bias, conv_gain, conv_nbias = (_rows(t) for t in (conv_dw_bias, conv_norm_gain, conv_norm_bias))
    a_log, dt_bias, out_norm = _rows(gdn_a_log, V7X_LANES), _rows(gdn_dt_bias, V7X_LANES), _rows(gdn_out_norm)
    x2d = x.reshape(T, D)
    for i in range(depth):
        *attn_parts, conv_u, gdn_qkv, gdn_gate, gdn_ab = _inproj(x2d, pre_gain, cos, sin, w_in_packed, i, S, tm)
        y_attn = _attention([t.reshape(B, S, -1) for t in attn_parts])
        y_conv = _conv_mixer(conv_u.reshape(B, S, -1), conv_dw, conv_bias, conv_gain, conv_nbias, i)
        y_gdn = _gdn_mixer(gdn_qkv.reshape(B, S, -1), gdn_gate.reshape(B, S, -1), gdn_ab.reshape(B, S, -1),
                           gdn_short_conv, a_log, dt_bias, out_norm, i)
        x2d = _out_ffn(x2d, y_attn.reshape(T, -1), y_conv.reshape(T, -1), y_gdn.reshape(T, -1), w_out_b,
                       post_gain, ffn_pre_gain, w_gate_b, w_up_b, w_down_b, ffn_post_gain, i, tm)
    return x2d.reshape(B, S, D)
```

```python
import math

import jax
import jax.numpy as jnp
from jax import lax
from jax.experimental import pallas as pl
from jax.experimental.pallas import tpu as pltpu

F32 = jnp.float32
BF16 = jnp.bfloat16

ATTN_HEAD_DIM = 64
ATTN_HEADS = 4
ATTN_WIDTH = ATTN_HEADS * ATTN_HEAD_DIM
ATTN_BLOCK = 128
ATTN_GROUP = 8
DILATIONS = (1, 4, 16)
ROPE_THETA = 10000.0
CONV_CH = 256
CONV_WIDTH = 31
GDN_HEADS = 4
GDN_HEAD_DIM = 128
GDN_WIDTH = GDN_HEADS * GDN_HEAD_DIM
GDN_CONV_WIDTH = 4
GDN_CHUNK = 128
RMS_EPS = 1e-6
LN_EPS = 1e-5
L2_EPS = 1e-6
NEG_BIG = -1e30
LOG2_E = math.log2(math.e)
LN_2 = math.log(2.0)

V7X_LANES = 128
V7X_SUBLANES = 8
V7X_VMEM_LIMIT_BYTES = 56 * 1024 * 1024

W_ATTN = 0
W_CONV = W_ATTN + 3 * ATTN_WIDTH
W_GQKV = W_CONV + 2 * CONV_CH
W_GGATE = W_GQKV + 3 * GDN_WIDTH
W_AB = W_GGATE + GDN_WIDTH
W_TOTAL = W_AB + V7X_LANES


def _rms(x):
    return x * lax.rsqrt(jnp.mean(x * x, axis=-1, keepdims=True) + RMS_EPS)


def _sigmoid(x):
    return 1.0 / (1.0 + jnp.exp(-x))


def _silu(x):
    return x * _sigmoid(x)


def _dot(a, b, precision=None):
    return jnp.dot(a, b, preferred_element_type=F32, precision=precision)


def _dot_nt(a, b, precision=None):
    return lax.dot_general(a, b, (((1,), (1,)), ((), ())), preferred_element_type=F32, precision=precision)


def _inproj_kernel(x_ref, gain_ref, cos_ref, sin_ref, w_ref, q1_ref, q2_ref, k1_ref, k2_ref, va_ref, vb_ref,
                   conv_ref, gqkv_ref, ggate_ref, ab_ref):
    h = (_rms(x_ref[...]) * gain_ref[...]).astype(BF16)

    def proj(lo, hi):
        return _dot(h, w_ref[:, lo:hi])

    qk = proj(W_ATTN, W_ATTN + 2 * ATTN_WIDTH)
    cos = cos_ref[...]
    sin = sin_ref[...]
    half = ATTN_WIDTH // 2
    q1, q2 = qk[:, 0:half], qk[:, half:2 * half]
    k1, k2 = qk[:, 2 * half:3 * half], qk[:, 3 * half:4 * half]
    scale = ATTN_HEAD_DIM ** -0.5 * LOG2_E
    q1_ref[...] = (q1 * cos - q2 * sin) * scale
    q2_ref[...] = (q2 * cos + q1 * sin) * scale
    k1_ref[...] = k1 * cos - k2 * sin
    k2_ref[...] = k2 * cos + k1 * sin
    v = proj(W_ATTN + 2 * ATTN_WIDTH, W_CONV)
    va_ref[...] = v[:, 0:half]
    vb_ref[...] = v[:, half:2 * half]
    conv_ref[...] = proj(W_CONV, W_GQKV)
    gqkv_ref[...] = proj(W_GQKV, W_GGATE)
    ggate_ref[...] = proj(W_GGATE, W_AB)
    ab_ref[...] = proj(W_AB, W_TOTAL)


def _layer_spec(stacked, layer, **kwargs):
    _, rows, cols = stacked.shape
    return pl.BlockSpec((None, rows, cols), lambda *_: (layer, 0, 0), **kwargs)


def _inproj(x2d, gain, cos, sin, w, layer, seq, tm):
    T, D = x2d.shape
    blocks_per_seq = seq // tm
    row = lambda i: (i, 0)
    widths = (ATTN_WIDTH // 2,) * 6 + (2 * CONV_CH, 3 * GDN_WIDTH, GDN_WIDTH, V7X_LANES)
    return pl.pallas_call(
        _inproj_kernel,
        name="inproj",
        grid=(T // tm,),
        in_specs=[
            pl.BlockSpec((tm, D), row),
            _layer_spec(gain, layer),
            pl.BlockSpec((tm, ATTN_WIDTH // 2), lambda i: (i % blocks_per_seq, 0)),
            pl.BlockSpec((tm, ATTN_WIDTH // 2), lambda i: (i % blocks_per_seq, 0)),
            _layer_spec(w, layer),
        ],
        out_specs=[pl.BlockSpec((tm, n), row) for n in widths],
        out_shape=[jax.ShapeDtypeStruct((T, n), F32) for n in widths],
        compiler_params=pltpu.CompilerParams(
            dimension_semantics=("arbitrary",), vmem_limit_bytes=V7X_VMEM_LIMIT_BYTES),
    )(x2d, gain, cos, sin, w)


def _attn_units(units):
    heads = range(ATTN_HEADS)
    lane = lax.broadcasted_iota(jnp.int32, (1, ATTN_WIDTH), 1)
    qk_head = (lane % (ATTN_WIDTH // 2)) // (ATTN_HEAD_DIM // 2)
    scores = []
    for q, k, _, _ in units:
        qb = q.astype(BF16)
        zero = jnp.zeros_like(qb)
        q_stack = jnp.concatenate([jnp.where(qk_head == h, qb, zero) for h in heads], axis=0)
        scores.append(_dot_nt(k.astype(BF16), q_stack))
    stats = []
    for s, (_, _, _, valid_t) in zip(scores, units):
        s = jnp.where(jnp.concatenate([valid_t] * ATTN_HEADS, axis=1), s, NEG_BIG)
        m = jnp.max(s, axis=0, keepdims=True)
        p = jnp.exp2(s - m)
        l = jnp.sum(p, axis=0, keepdims=True)
        stats.append((p.astype(BF16), 1.0 / l, (m + jnp.log2(l)) * LN_2))
    outs = [lax.dot_general(v.astype(BF16), p, (((0,), (0,)), ((), ())), preferred_element_type=F32)
            for (p, _, _), (_, _, v, _) in zip(stats, units)]
    results = []
    for o_t, (_, inv_l, lse), (q, _, _, _) in zip(outs, stats, units):
        nq = q.shape[0]
        d = ATTN_HEAD_DIM
        out_t = jnp.concatenate(
            [o_t[h * d:(h + 1) * d, h * nq:(h + 1) * nq] * inv_l[:, h * nq:(h + 1) * nq] for h in heads], axis=0)
        lse_t = jnp.concatenate(
            [jnp.broadcast_to(lse[:, h * nq:(h + 1) * nq], (d, nq)) for h in heads], axis=0)
        results.append((out_t.T, lse_t.T))
    return results


def _attn_kernel(q1_ref, q2_ref, k1_ref, k2_ref, va_ref, vb_ref, o_ref,
                 o2a_ref, o2b_ref, l2a_ref, l2b_ref, o3a_ref, o3b_ref, l3a_ref, l3b_ref):
    n = ATTN_BLOCK
    seq = q1_ref.shape[0]
    ki = lax.broadcasted_iota(jnp.int32, (2 * n, n), 0)
    qi = lax.broadcasted_iota(jnp.int32, (2 * n, n), 1)
    valid_band = (ki >= qi) & (ki <= qi + n)
    valid_first = lax.broadcasted_iota(jnp.int32, (n, n), 0) <= lax.broadcasted_iota(jnp.int32, (n, n), 1)

    def rows(start, size, stride):
        if stride == 1:
            return pl.ds(start, size)
        return pl.ds(start, size, stride=stride)

    def load(refs, sl):
        return jnp.concatenate([r[sl, :] for r in refs], axis=1)

    def store(refs, sl, val):
        for j, r in enumerate(refs):
            r[sl, :] = val[:, j * V7X_LANES:(j + 1) * V7X_LANES]

    def run_units(specs):
        units = []
        for q_start, stride, first in specs:
            q = load((q1_ref, q2_ref), rows(q_start, n, stride))
            if first:
                k_rows = rows(q_start, n, stride)
                valid = valid_first
            else:
                k_rows = rows(q_start - n * stride, 2 * n, stride)
                valid = valid_band
            units.append((q, load((k1_ref, k2_ref), k_rows), load((va_ref, vb_ref), k_rows), valid))
        return _attn_units(units)

    def run_dilated(specs, o_refs, l_refs):
        for (q_start, stride, _), (o, lse) in zip(specs, run_units(specs)):
            store(o_refs, rows(q_start, n, stride), o)
            store(l_refs, rows(q_start, n, stride), lse)

    d2, d3 = DILATIONS[1], DILATIONS[2]
    blocks2 = seq // (n * d2)
    residues2 = ATTN_GROUP // blocks2
    assert seq == n * d3 and d3 % ATTN_GROUP == 0 and ATTN_GROUP % blocks2 == 0 and d2 % residues2 == 0
    assert (seq // n) % ATTN_GROUP == 0

    def widest_body(g, carry):
        specs = [(g * ATTN_GROUP + u, d3, True) for u in range(ATTN_GROUP)]
        run_dilated(specs, (o3a_ref, o3b_ref), (l3a_ref, l3b_ref))
        return carry

    lax.fori_loop(0, d3 // ATTN_GROUP, widest_body, 0)

    def middle_body(g, carry):
        specs = [(b * n * d2 + g * residues2 + j, d2, b == 0) for j in range(residues2) for b in range(blocks2)]
        run_dilated(specs, (o2a_ref, o2b_ref), (l2a_ref, l2b_ref))
        return carry

    lax.fori_loop(0, d2 // residues2, middle_body, 0)

    def run_dense(specs):
        for (start, _, _), (o1, l1) in zip(specs, run_units(specs)):
            sl = pl.ds(start, n)
            o2, l2 = load((o2a_ref, o2b_ref), sl), load((l2a_ref, l2b_ref), sl)
            o3, l3 = load((o3a_ref, o3b_ref), sl), load((l3a_ref, l3b_ref), sl)
            top = jnp.maximum(jnp.maximum(l1, l2), l3)
            w1, w2, w3 = jnp.exp(l1 - top), jnp.exp(l2 - top), jnp.exp(l3 - top)
            o_ref[sl, :] = (w1 * o1 + w2 * o2 + w3 * o3) / (w1 + w2 + w3)

    run_dense([(u * n, 1, u == 0) for u in range(ATTN_GROUP)])

    def dense_body(g, carry):
        base = pl.multiple_of(g * (ATTN_GROUP * n), ATTN_GROUP * n)
        run_dense([(base + u * n, 1, False) for u in range(ATTN_GROUP)])
        return carry

    lax.fori_loop(1, seq // (ATTN_GROUP * n), dense_body, 0)


def _attention(parts):
    B, S, W = parts[0].shape
    blk = lambda b: (b, 0, 0)
    return pl.pallas_call(
        _attn_kernel,
        name="dilated_attn",
        grid=(B,),
        in_specs=[pl.BlockSpec((None, S, W), blk) for _ in parts],
        out_specs=pl.BlockSpec((None, S, ATTN_WIDTH), blk),
        out_shape=jax.ShapeDtypeStruct((B, S, ATTN_WIDTH), F32),
        scratch_shapes=[pltpu.VMEM((S, V7X_LANES), F32) for _ in range(8)],
        compiler_params=pltpu.CompilerParams(
            dimension_semantics=("arbitrary",), vmem_limit_bytes=V7X_VMEM_LIMIT_BYTES),
    )(*parts)


CONV_PAD = 32
CONV_ROWS = 2048


def _conv_kernel(u_ref, dw_ref, bias_ref, gain_ref, nbias_ref, o_ref, g_ref):
    seq = u_ref.shape[0]
    g_ref[0:CONV_PAD, :] = jnp.zeros((CONV_PAD, CONV_CH), F32)
    g_ref[CONV_PAD:CONV_PAD + seq, :] = u_ref[:, 0:CONV_CH] * _sigmoid(u_ref[:, CONV_CH:2 * CONV_CH])
    dw = dw_ref[...]
    bias = bias_ref[...]
    gain = gain_ref[...]
    nbias = nbias_ref[...]
    lead = CONV_PAD - (CONV_WIDTH - 1)
    sub = V7X_SUBLANES

    def body(c, carry):
        r0 = pl.multiple_of(c * CONV_ROWS, CONV_ROWS)
        acc = jnp.zeros((CONV_ROWS, CONV_CH), F32) + bias
        for rem in range(sub):
            taps = [j for j in range(CONV_WIDTH) if (lead + j) % sub == rem]
            nrows = CONV_ROWS if rem == 0 else CONV_ROWS + sub
            part = None
            for j in taps:
                start = pl.multiple_of(r0 + (lead + j - rem), sub)
                term = dw[j:j + 1, :] * g_ref[pl.ds(start, nrows), :]
                part = term if part is None else part + term
            if rem:
                part = pltpu.roll(part, nrows - rem, axis=0)[0:CONV_ROWS, :]
            acc = acc + part
        mu = jnp.mean(acc, axis=-1, keepdims=True)
        cen = acc - mu
        var = jnp.mean(cen * cen, axis=-1, keepdims=True)
        y = cen * lax.rsqrt(var + LN_EPS) * gain + nbias
        o_ref[pl.ds(r0, CONV_ROWS), :] = _silu(y)
        return carry

    lax.fori_loop(0, seq // CONV_ROWS, body, 0)


def _conv_mixer(u, dw, bias, gain, nbias, layer):
    B, S, W = u.shape
    blk = lambda b: (b, 0, 0)
    return pl.pallas_call(
        _conv_kernel,
        name="conformer_conv",
        grid=(B,),
        in_specs=[pl.BlockSpec((None, S, W), blk)] + [_layer_spec(t, layer) for t in (dw, bias, gain, nbias)],
        out_specs=pl.BlockSpec((None, S, CONV_CH), blk),
        out_shape=jax.ShapeDtypeStruct((B, S, CONV_CH), F32),
        scratch_shapes=[pltpu.VMEM((CONV_PAD + S, CONV_CH), F32)],
        compiler_params=pltpu.CompilerParams(
            dimension_semantics=("arbitrary",), vmem_limit_bytes=V7X_VMEM_LIMIT_BYTES),
    )(u, dw, bias, gain, nbias)


GDN_SEQ_BLOCK = 256
GDN_SEQS = 2
HIGHEST = lax.Precision.HIGHEST


def _unit_lower_inverses(m_list, rows, cols):
    size = m_list[0].shape[0]
    eye = (rows == cols).astype(F32)
    link = (rows // 2 == cols // 2) & (rows == cols + 1)
    ts = [eye - jnp.where(link, m, 0.0) for m in m_list]
    neg_mbs = [(-m).astype(BF16) for m in m_list]
    step = 2
    while step < size:
        link = ((rows // step) == (cols // step) + 1) & ((rows // (2 * step)) == (cols // (2 * step)))
        tbs = [t.astype(BF16) for t in ts]
        mts = [_dot(mb, tb).astype(BF16) for mb, tb in zip(neg_mbs, tbs)]
        ts = [jnp.where(link, _dot(tb, mt), t) for t, tb, mt in zip(ts, tbs, mts)]
        step *= 2
    return ts


def _gdn_kernel(qkv_ref, gate_ref, ab_ref, cw_ref, alog_ref, dtb_ref, onorm_ref, o_ref, state_ref, halo_ref):
    C = GDN_CHUNK
    D = GDN_HEAD_DIM
    W = GDN_WIDTH
    nseq, rows_blk = qkv_ref.shape[0], qkv_ref.shape[1]
    nchunks = rows_blk // C
    halo = halo_ref.shape[1]
    last = GDN_CONV_WIDTH - 1
    heads = range(GDN_HEADS)

    @pl.when(pl.program_id(1) == 0)
    def _():
        state_ref[...] = jnp.zeros_like(state_ref)
        halo_ref[...] = jnp.zeros_like(halo_ref)

    cw = cw_ref[...]
    alog = alog_ref[...]
    dtb = dtb_ref[...]
    onorm = onorm_ref[...]
    ri = lax.broadcasted_iota(jnp.int32, (C, C), 0)
    ci = lax.broadcasted_iota(jnp.int32, (C, C), 1)
    lower_incl = ri >= ci
    strict = ri > ci
    tril_ones = lower_incl.astype(F32)

    chains = []
    for b in range(nseq):
        for c in range(nchunks):
            r0 = c * C
            prev = halo_ref[b] if c == 0 else qkv_ref[b, r0 - halo:r0, :]
            win = jnp.concatenate([prev, qkv_ref[b, r0:r0 + C, :]], axis=0)
            acc = cw[last:last + 1, :] * win[halo:, :]
            for j in range(last):
                acc = acc + cw[j:j + 1, :] * pltpu.roll(win, last - j, axis=0)[halo:, :]
            qkv = _silu(acc)

            ab = ab_ref[b, r0:r0 + C, :]
            xa = ab + dtb
            softplus = jnp.maximum(xa, 0.0) + jnp.log1p(jnp.exp(-jnp.abs(xa)))
            g = -jnp.exp(alog) * softplus
            gc = _dot(tril_ones, g, HIGHEST)
            gc_t = gc.T
            beta_all = _sigmoid(ab)

            for h in heads:
                q = qkv[:, h * D:(h + 1) * D]
                k = qkv[:, W + h * D:W + (h + 1) * D]
                v = qkv[:, 2 * W + h * D:2 * W + (h + 1) * D]
                q = q * lax.rsqrt(jnp.sum(q * q, axis=-1, keepdims=True) + L2_EPS) * (D ** -0.5)
                k = k * lax.rsqrt(jnp.sum(k * k, axis=-1, keepdims=True) + L2_EPS)
                g_col = gc[:, h:h + 1]
                g_row = gc_t[h:h + 1, :]
                beta = beta_all[:, GDN_HEADS + h:GDN_HEADS + h + 1]
                decay = jnp.exp(jnp.where(lower_incl, g_col - g_row, NEG_BIG))
                kb = k * beta
                g_last = g_col[C - 1:C, :]
                chains.append(dict(
                    decay=decay,
                    kb=kb.astype(BF16),
                    k=k.astype(BF16),
                    q=q.astype(BF16),
                    rhs=jnp.concatenate([v * beta, kb * jnp.exp(g_col)], axis=1).astype(BF16),
                    q_dec=(q * jnp.exp(g_col)).astype(BF16),
                    k_dec_t=(k * jnp.exp(g_last - g_col)).T.astype(BF16),
                    carry=jnp.exp(g_last),
                ))

    m_list = [jnp.where(strict, _dot_nt(p["kb"], p["k"]) * p["decay"], 0.0) for p in chains]
    t_list = _unit_lower_inverses(m_list, ri, ci)
    sols = [_dot(t.astype(BF16), p["rhs"]) for t, p in zip(t_list, chains)]
    intras = [(_dot_nt(p["q"], p["k"]) * p["decay"]).astype(BF16) for p in chains]

    lanes = [(b, h) for b in range(nseq) for h in heads]
    states = [state_ref[b, h] for b, h in lanes]
    for c in range(nchunks):
        r0 = c * C
        idx = [(b * nchunks + c) * GDN_HEADS + h for b, h in lanes]
        sbs = [s.astype(BF16) for s in states]
        v_news = [(sols[i][:, :D] - _dot(sols[i][:, D:].astype(BF16), sb)).astype(BF16) for i, sb in zip(idx, sbs)]
        outs = [_dot(chains[i]["q_dec"], sb) + _dot(intras[i], vn) for i, sb, vn in zip(idx, sbs, v_news)]
        states = [s * chains[i]["carry"] + _dot(chains[i]["k_dec_t"], vn) for i, s, vn in zip(idx, states, v_news)]
        for (b, h), out in zip(lanes, outs):
            y = _rms(out) * onorm * _silu(gate_ref[b, r0:r0 + C, h * D:(h + 1) * D])
            o_ref[b, r0:r0 + C, h * D:(h + 1) * D] = y
    for (b, h), s in zip(lanes, states):
        state_ref[b, h] = s
    for b in range(nseq):
        halo_ref[b] = qkv_ref[b, rows_blk - halo:rows_blk, :]


def _gdn_mixer(qkv, gate, ab, conv_w, a_log, dt_bias, out_norm, layer):
    B, S, _ = qkv.shape
    sb = GDN_SEQ_BLOCK
    ns = GDN_SEQS
    assert B % ns == 0 and S % sb == 0
    blk = lambda b, s: (b, s, 0)
    return pl.pallas_call(
        _gdn_kernel,
        name="gated_deltanet",
        grid=(B // ns, S // sb),
        in_specs=[
            pl.BlockSpec((ns, sb, 3 * GDN_WIDTH), blk),
            pl.BlockSpec((ns, sb, GDN_WIDTH), blk),
            pl.BlockSpec((ns, sb, V7X_LANES), blk),
        ] + [_layer_spec(t, layer) for t in (conv_w, a_log, dt_bias, out_norm)],
        out_specs=pl.BlockSpec((ns, sb, GDN_WIDTH), blk),
        out_shape=jax.ShapeDtypeStruct((B, S, GDN_WIDTH), F32),
        scratch_shapes=[
            pltpu.VMEM((ns, GDN_HEADS, GDN_HEAD_DIM, GDN_HEAD_DIM), F32),
            pltpu.VMEM((ns, V7X_SUBLANES, 3 * GDN_WIDTH), F32),
        ],
        compiler_params=pltpu.CompilerParams(
            dimension_semantics=("arbitrary", "arbitrary"), vmem_limit_bytes=V7X_VMEM_LIMIT_BYTES),
    )(qkv, gate, ab, conv_w, a_log, dt_bias, out_norm)


def _out_ffn_kernel(x_ref, attn_ref, conv_ref, gdn_ref, wo_ref, g_post_ref, g_pre_ref, wg_ref, wu_ref, wd_ref,
                    g_ffn_ref, o_ref):
    a0, a1, a2 = ATTN_WIDTH, ATTN_WIDTH + CONV_CH, ATTN_WIDTH + CONV_CH + GDN_WIDTH
    tm = x_ref.shape[0]
    halves = [(0, tm // 2), (tm // 2, tm)]
    ys = []
    for lo, hi in halves:
        y = _dot(attn_ref[lo:hi, :].astype(BF16), wo_ref[0:a0, :])
        y = y + _dot(conv_ref[lo:hi, :].astype(BF16), wo_ref[a0:a1, :])
        y = y + _dot(gdn_ref[lo:hi, :].astype(BF16), wo_ref[a1:a2, :])
        ys.append(y)
    x1s = [x_ref[lo:hi, :] + _rms(y) * g_post_ref[...] for (lo, hi), y in zip(halves, ys)]
    hs = [(_rms(x1) * g_pre_ref[...]).astype(BF16) for x1 in x1s]
    acts = [(_silu(_dot(h, wg_ref[...])) * _dot(h, wu_ref[...])).astype(BF16) for h in hs]
    fs = [_dot(act, wd_ref[...]) for act in acts]
    for (lo, hi), x1, f in zip(halves, x1s, fs):
        o_ref[lo:hi, :] = x1 + _rms(f) * g_ffn_ref[...]


def _out_ffn(x2d, attn, conv, gdn, wo, g_post, g_pre, wg, wu, wd, g_ffn, layer, tm):
    T, D = x2d.shape
    row = lambda i: (i, 0)
    resident = lambda t: _layer_spec(t, layer, pipeline_mode=pl.Buffered(1))
    return pl.pallas_call(
        _out_ffn_kernel,
        name="outproj_ffn",
        grid=(T // tm,),
        in_specs=[
            pl.BlockSpec((tm, D), row),
            pl.BlockSpec((tm, ATTN_WIDTH), row),
            pl.BlockSpec((tm, CONV_CH), row),
            pl.BlockSpec((tm, GDN_WIDTH), row),
            resident(wo),
            _layer_spec(g_post, layer),
            _layer_spec(g_pre, layer),
            resident(wg),
            resident(wu),
            resident(wd),
            _layer_spec(g_ffn, layer),
        ],
        out_specs=pl.BlockSpec((tm, D), row),
        out_shape=jax.ShapeDtypeStruct((T, D), F32),
        compiler_params=pltpu.CompilerParams(
            dimension_semantics=("arbitrary",), vmem_limit_bytes=V7X_VMEM_LIMIT_BYTES),
    )(x2d, attn, conv, gdn, wo, g_post, g_pre, wg, wu, wd, g_ffn)


def _pack_in_weight(w_in):
    depth, d_model, in_width = w_in.shape
    wb = w_in.astype(BF16)
    half = ATTN_HEAD_DIM // 2
    qk = wb[:, :, :2 * ATTN_WIDTH].reshape(depth, d_model, 2, ATTN_HEADS, 2, half)
    qk = qk.transpose(0, 1, 2, 4, 3, 5).reshape(depth, d_model, 2 * ATTN_WIDTH)
    pad = jnp.zeros((depth, d_model, W_TOTAL - in_width), BF16)
    return jnp.concatenate([qk, wb[:, :, 2 * ATTN_WIDTH:], pad], axis=-1)


def _rotary_tables(seq):
    half = ATTN_HEAD_DIM // 2
    inv_freq = jnp.exp(-math.log(ROPE_THETA) * jnp.arange(half, dtype=F32) * (2.0 / ATTN_HEAD_DIM))
    ang = jnp.arange(seq).astype(F32)[:, None] * inv_freq[None, :]
    return jnp.tile(jnp.cos(ang), (1, ATTN_HEADS)), jnp.tile(jnp.sin(ang), (1, ATTN_HEADS))


def _rows(t, lanes=None):
    if lanes is not None:
        t = jnp.pad(t, ((0, 0), (0, lanes - t.shape[-1])))
    return t[:, None, :]


def kernel(x, attn_pre_norm, w_in, conv_dw, conv_dw_bias, conv_norm_gain, conv_norm_bias, gdn_short_conv, gdn_a_log,
           gdn_dt_bias, gdn_out_norm, w_out, attn_post_norm, ffn_pre_norm, w_gate, w_up, w_down, ffn_post_norm):
    B, S, D = x.shape
    depth = w_in.shape[0]
    T = B * S
    tm = 512
    w_in_packed = _pack_in_weight(w_in)
    w_out_b, w_gate_b, w_up_b, w_down_b = (t.astype(BF16) for t in (w_out, w_gate, w_up, w_down))
    cos, sin = _rotary_tables(S)
    pre_gain, post_gain, ffn_pre_gain, ffn_post_gain = (
        _rows(t) for t in (attn_pre_norm, attn_post_norm, ffn_pre_norm, ffn_post_norm))
    conv_bias, conv_gain, conv_nbias = (_rows(t) for t in (conv_dw_bias, conv_norm_gain, conv_norm_bias))
    a_log, dt_bias, out_norm = _rows(gdn_a_log, V7X_LANES), _rows(gdn_dt_bias, V7X_LANES), _rows(gdn_out_norm)
    x2d = x.reshape(T, D)
    for i in range(depth):
        *attn_parts, conv_u, gdn_qkv, gdn_gate, gdn_ab = _inproj(x2d, pre_gain, cos, sin, w_in_packed, i, S, tm)
        y_attn = _attention([t.reshape(B, S, -1) for t in attn_parts])
        y_conv = _conv_mixer(conv_u.reshape(B, S, -1), conv_dw, conv_bias, conv_gain, conv_nbias, i)
        y_gdn = _gdn_mixer(gdn_qkv.reshape(B, S, -1), gdn_gate.reshape(B, S, -1), gdn_ab.reshape(B, S, -1),
                           gdn_short_conv, a_log, dt_bias, out_norm, i)
        x2d = _out_ffn(x2d, y_attn.reshape(T, -1), y_conv.reshape(T, -1), y_gdn.reshape(T, -1), w_out_b,
                       post_gain, ffn_pre_gain, w_gate_b, w_up_b, w_down_b, ffn_post_gain, i, tm)
    return x2d.reshape(B, S, D)
```

```python
import math

import jax
import jax.numpy as jnp
from jax import lax
from jax.experimental import pallas as pl
from jax.experimental.pallas import tpu as pltpu

F32 = jnp.float32
BF16 = jnp.bfloat16

ATTN_HEAD_DIM = 64
ATTN_HEADS = 4
ATTN_WIDTH = ATTN_HEADS * ATTN_HEAD_DIM
ATTN_BLOCK = 128
ATTN_GROUP = 8
DILATIONS = (1, 4, 16)
ROPE_THETA = 10000.0
CONV_CH = 256
CONV_WIDTH = 31
GDN_HEADS = 4
GDN_HEAD_DIM = 128
GDN_WIDTH = GDN_HEADS * GDN_HEAD_DIM
GDN_CONV_WIDTH = 4
GDN_CHUNK = 128
RMS_EPS = 1e-6
LN_EPS = 1e-5
L2_EPS = 1e-6
NEG_BIG = -1e30
LOG2_E = math.log2(math.e)

ROW_TILE = 512

V7X_LANES = 128
V7X_SUBLANES = 8
V7X_VMEM_LIMIT_BYTES = 56 * 1024 * 1024

W_ATTN = 0
W_CONV = W_ATTN + 3 * ATTN_WIDTH
W_GQKV = W_CONV + 2 * CONV_CH
W_GGATE = W_GQKV + 3 * GDN_WIDTH
W_AB = W_GGATE + GDN_WIDTH
W_TOTAL = W_AB + V7X_LANES


def _rms(x):
    return x * lax.rsqrt(jnp.mean(x * x, axis=-1, keepdims=True) + RMS_EPS)


def _sigmoid(x):
    return 1.0 / (1.0 + jnp.exp(-x))


def _silu(x):
    return x * _sigmoid(x)


def _dot(a, b, precision=None):
    return jnp.dot(a, b, preferred_element_type=F32, precision=precision)


def _dot_nt(a, b, precision=None):
    return lax.dot_general(a, b, (((1,), (1,)), ((), ())), preferred_element_type=F32, precision=precision)


def _inproj_kernel(x_ref, gain_ref, cos_ref, sin_ref, w_ref, q1_ref, q2_ref, k1_ref, k2_ref, va_ref, vb_ref,
                   conv_ref, gqkv_ref, ggate_ref, ab_ref):
    h = (_rms(x_ref[...]) * gain_ref[...]).astype(BF16)

    def proj(lo, hi):
        return _dot(h, w_ref[:, lo:hi])

    qk = proj(W_ATTN, W_ATTN + 2 * ATTN_WIDTH)
    cos = cos_ref[...]
    sin = sin_ref[...]
    half = ATTN_WIDTH // 2
    q1, q2 = qk[:, 0:half], qk[:, half:2 * half]
    k1, k2 = qk[:, 2 * half:3 * half], qk[:, 3 * half:4 * half]
    scale = ATTN_HEAD_DIM ** -0.5 * LOG2_E
    q1_ref[...] = (q1 * cos - q2 * sin) * scale
    q2_ref[...] = (q2 * cos + q1 * sin) * scale
    k1_ref[...] = k1 * cos - k2 * sin
    k2_ref[...] = k2 * cos + k1 * sin
    v = proj(W_ATTN + 2 * ATTN_WIDTH, W_CONV)
    va_ref[...] = v[:, 0:half]
    vb_ref[...] = v[:, half:2 * half]
    conv_ref[...] = proj(W_CONV, W_GQKV)
    gqkv_ref[...] = proj(W_GQKV, W_GGATE)
    ggate_ref[...] = proj(W_GGATE, W_AB)
    ab_ref[...] = proj(W_AB, W_TOTAL)


def _layer_spec(stacked, layer, **kwargs):
    _, rows, cols = stacked.shape
    return pl.BlockSpec((None, rows, cols), lambda *_: (layer, 0, 0), **kwargs)


def _inproj(x2d, gain, cos, sin, w, layer, seq, tm):
    T, D = x2d.shape
    blocks_per_seq = seq // tm
    row = lambda i: (i, 0)
    widths = (ATTN_WIDTH // 2,) * 6 + (2 * CONV_CH, 3 * GDN_WIDTH, GDN_WIDTH, V7X_LANES)
    return pl.pallas_call(
        _inproj_kernel,
        name="inproj",
        grid=(T // tm,),
        in_specs=[
            pl.BlockSpec((tm, D), row),
            _layer_spec(gain, layer),
            pl.BlockSpec((tm, ATTN_WIDTH // 2), lambda i: (i % blocks_per_seq, 0)),
            pl.BlockSpec((tm, ATTN_WIDTH // 2), lambda i: (i % blocks_per_seq, 0)),
            _layer_spec(w, layer),
        ],
        out_specs=[pl.BlockSpec((tm, n), row) for n in widths],
        out_shape=[jax.ShapeDtypeStruct((T, n), F32) for n in widths],
        compiler_params=pltpu.CompilerParams(
            dimension_semantics=("arbitrary",), vmem_limit_bytes=V7X_VMEM_LIMIT_BYTES),
    )(x2d, gain, cos, sin, w)


def _attn_units(units):
    heads = range(ATTN_HEADS)
    lane = lax.broadcasted_iota(jnp.int32, (1, ATTN_WIDTH), 1)
    qk_head = (lane % (ATTN_WIDTH // 2)) // (ATTN_HEAD_DIM // 2)
    scores = []
    for q, k, _, _ in units:
        qb = q.astype(BF16)
        zero = jnp.zeros_like(qb)
        q_stack = jnp.concatenate([jnp.where(qk_head == h, qb, zero) for h in heads], axis=0)
        scores.append(_dot_nt(k.astype(BF16), q_stack))
    stats = []
    for s, (_, _, _, valid_t) in zip(scores, units):
        s = jnp.where(jnp.concatenate([valid_t] * ATTN_HEADS, axis=1), s, NEG_BIG)
        m = jnp.max(s, axis=0, keepdims=True)
        p = jnp.exp2(s - m)
        l = jnp.sum(p, axis=0, keepdims=True)
        stats.append((p.astype(BF16), 1.0 / l, m + jnp.log2(l)))
    outs = [lax.dot_general(v.astype(BF16), p, (((0,), (0,)), ((), ())), preferred_element_type=F32)
            for (p, _, _), (_, _, v, _) in zip(stats, units)]
    results = []
    for o_t, (_, inv_l, lse), (q, _, _, _) in zip(outs, stats, units):
        nq = q.shape[0]
        d = ATTN_HEAD_DIM
        out_t = jnp.concatenate(
            [o_t[h * d:(h + 1) * d, h * nq:(h + 1) * nq] * inv_l[:, h * nq:(h + 1) * nq] for h in heads], axis=0)
        lse_t = jnp.concatenate(
            [jnp.broadcast_to(lse[:, h * nq:(h + 1) * nq], (d, nq)) for h in heads], axis=0)
        results.append((out_t.T, lse_t.T))
    return results


def _attn_kernel(q1_ref, q2_ref, k1_ref, k2_ref, va_ref, vb_ref, o_ref,
                 o2a_ref, o2b_ref, l2a_ref, l2b_ref, o3a_ref, o3b_ref, l3a_ref, l3b_ref):
    n = ATTN_BLOCK
    seq = q1_ref.shape[0]
    ki = lax.broadcasted_iota(jnp.int32, (2 * n, n), 0)
    qi = lax.broadcasted_iota(jnp.int32, (2 * n, n), 1)
    valid_band = (ki >= qi) & (ki <= qi + n)
    valid_first = lax.broadcasted_iota(jnp.int32, (n, n), 0) <= lax.broadcasted_iota(jnp.int32, (n, n), 1)

    def rows(start, size, stride):
        if stride == 1:
            return pl.ds(start, size)
        return pl.ds(start, size, stride=stride)

    def load(refs, sl):
        return jnp.concatenate([r[sl, :] for r in refs], axis=1)

    def store(refs, sl, val):
        for j, r in enumerate(refs):
            r[sl, :] = val[:, j * V7X_LANES:(j + 1) * V7X_LANES]

    def run_units(specs):
        units = []
        for q_start, stride, first in specs:
            q = load((q1_ref, q2_ref), rows(q_start, n, stride))
            if first:
                k_rows = rows(q_start, n, stride)
                valid = valid_first
            else:
                k_rows = rows(q_start - n * stride, 2 * n, stride)
                valid = valid_band
            units.append((q, load((k1_ref, k2_ref), k_rows), load((va_ref, vb_ref), k_rows), valid))
        return _attn_units(units)

    def run_dilated(specs, o_refs, l_refs):
        for (q_start, stride, _), (o, lse) in zip(specs, run_units(specs)):
            store(o_refs, rows(q_start, n, stride), o)
            store(l_refs, rows(q_start, n, stride), lse)

    d2, d3 = DILATIONS[1], DILATIONS[2]
    blocks2 = seq // (n * d2)
    residues2 = ATTN_GROUP // blocks2
    assert seq == n * d3 and d3 % ATTN_GROUP == 0 and ATTN_GROUP % blocks2 == 0 and d2 % residues2 == 0
    assert (seq // n) % ATTN_GROUP == 0

    def widest_body(g, carry):
        specs = [(g * ATTN_GROUP + u, d3, True) for u in range(ATTN_GROUP)]
        run_dilated(specs, (o3a_ref, o3b_ref), (l3a_ref, l3b_ref))
        return carry

    lax.fori_loop(0, d3 // ATTN_GROUP, widest_body, 0)

    def middle_body(g, carry):
        specs = [(b * n * d2 + g * residues2 + j, d2, b == 0) for j in range(residues2) for b in range(blocks2)]
        run_dilated(specs, (o2a_ref, o2b_ref), (l2a_ref, l2b_ref))
        return carry

    lax.fori_loop(0, d2 // residues2, middle_body, 0)

    def run_dense(specs):
        for (start, _, _), (o1, l1) in zip(specs, run_units(specs)):
            sl = pl.ds(start, n)
            o2, l2 = load((o2a_ref, o2b_ref), sl), load((l2a_ref, l2b_ref), sl)
            o3, l3 = load((o3a_ref, o3b_ref), sl), load((l3a_ref, l3b_ref), sl)
            top = jnp.maximum(jnp.maximum(l1, l2), l3)
            w1, w2, w3 = jnp.exp2(l1 - top), jnp.exp2(l2 - top), jnp.exp2(l3 - top)
            o_ref[sl, :] = (w1 * o1 + w2 * o2 + w3 * o3) / (w1 + w2 + w3)

    run_dense([(u * n, 1, u == 0) for u in range(ATTN_GROUP)])

    def dense_body(g, carry):
        base = pl.multiple_of(g * (ATTN_GROUP * n), ATTN_GROUP * n)
        run_dense([(base + u * n, 1, False) for u in range(ATTN_GROUP)])
        return carry

    lax.fori_loop(1, seq // (ATTN_GROUP * n), dense_body, 0)


def _attention(parts):
    B, S, W = parts[0].shape
    blk = lambda b: (b, 0, 0)
    return pl.pallas_call(
        _attn_kernel,
        name="dilated_attn",
        grid=(B,),
        in_specs=[pl.BlockSpec((None, S, W), blk) for _ in parts],
        out_specs=pl.BlockSpec((None, S, ATTN_WIDTH), blk),
        out_shape=jax.ShapeDtypeStruct((B, S, ATTN_WIDTH), F32),
        scratch_shapes=[pltpu.VMEM((S, V7X_LANES), F32) for _ in range(8)],
        compiler_params=pltpu.CompilerParams(
            dimension_semantics=("arbitrary",), vmem_limit_bytes=V7X_VMEM_LIMIT_BYTES),
    )(*parts)


CONV_PAD = 32
CONV_ROWS = 2048


def _conv_kernel(u_ref, dw_ref, bias_ref, gain_ref, nbias_ref, o_ref, g_ref):
    seq = u_ref.shape[0]
    g_ref[0:CONV_PAD, :] = jnp.zeros((CONV_PAD, CONV_CH), F32)
    g_ref[CONV_PAD:CONV_PAD + seq, :] = u_ref[:, 0:CONV_CH] * _sigmoid(u_ref[:, CONV_CH:2 * CONV_CH])
    dw = dw_ref[...]
    bias = bias_ref[...]
    gain = gain_ref[...]
    nbias = nbias_ref[...]
    lead = CONV_PAD - (CONV_WIDTH - 1)
    sub = V7X_SUBLANES

    def body(c, carry):
        r0 = pl.multiple_of(c * CONV_ROWS, CONV_ROWS)
        acc = jnp.zeros((CONV_ROWS, CONV_CH), F32) + bias
        for rem in range(sub):
            taps = [j for j in range(CONV_WIDTH) if (lead + j) % sub == rem]
            nrows = CONV_ROWS if rem == 0 else CONV_ROWS + sub
            part = None
            for j in taps:
                start = pl.multiple_of(r0 + (lead + j - rem), sub)
                term = dw[j:j + 1, :] * g_ref[pl.ds(start, nrows), :]
                part = term if part is None else part + term
            if rem:
                part = pltpu.roll(part, nrows - rem, axis=0)[0:CONV_ROWS, :]
            acc = acc + part
        mu = jnp.mean(acc, axis=-1, keepdims=True)
        cen = acc - mu
        var = jnp.mean(cen * cen, axis=-1, keepdims=True)
        y = cen * lax.rsqrt(var + LN_EPS) * gain + nbias
        o_ref[pl.ds(r0, CONV_ROWS), :] = _silu(y)
        return carry

    lax.fori_loop(0, seq // CONV_ROWS, body, 0)


def _conv_mixer(u, dw, bias, gain, nbias, layer):
    B, S, W = u.shape
    blk = lambda b: (b, 0, 0)
    return pl.pallas_call(
        _conv_kernel,
        name="conformer_conv",
        grid=(B,),
        in_specs=[pl.BlockSpec((None, S, W), blk)] + [_layer_spec(t, layer) for t in (dw, bias, gain, nbias)],
        out_specs=pl.BlockSpec((None, S, CONV_CH), blk),
        out_shape=jax.ShapeDtypeStruct((B, S, CONV_CH), F32),
        scratch_shapes=[pltpu.VMEM((CONV_PAD + S, CONV_CH), F32)],
        compiler_params=pltpu.CompilerParams(
            dimension_semantics=("arbitrary",), vmem_limit_bytes=V7X_VMEM_LIMIT_BYTES),
    )(u, dw, bias, gain, nbias)


GDN_SEQ_BLOCK = 256
GDN_SEQS = 2
HIGHEST = lax.Precision.HIGHEST


def _unit_lower_inverses(m_list, rows, cols):
    size = m_list[0].shape[0]
    eye = (rows == cols).astype(F32)
    link = (rows // 2 == cols // 2) & (rows == cols + 1)
    ts = [eye - jnp.where(link, m, 0.0) for m in m_list]
    neg_mbs = [(-m).astype(BF16) for m in m_list]
    step = 2
    while step < size:
        link = ((rows // step) == (cols // step) + 1) & ((rows // (2 * step)) == (cols // (2 * step)))
        tbs = [t.astype(BF16) for t in ts]
        mts = [_dot(mb, tb).astype(BF16) for mb, tb in zip(neg_mbs, tbs)]
        ts = [jnp.where(link, _dot(tb, mt), t) for t, tb, mt in zip(ts, tbs, mts)]
        step *= 2
    return ts


def _gdn_kernel(qkv_ref, gate_ref, ab_ref, cw_ref, alog_ref, dtb_ref, onorm_ref, o_ref, state_ref, halo_ref):
    C = GDN_CHUNK
    D = GDN_HEAD_DIM
    W = GDN_WIDTH
    nseq, rows_blk = qkv_ref.shape[0], qkv_ref.shape[1]
    nchunks = rows_blk // C
    halo = halo_ref.shape[1]
    last = GDN_CONV_WIDTH - 1
    heads = range(GDN_HEADS)

    @pl.when(pl.program_id(1) == 0)
    def _():
        state_ref[...] = jnp.zeros_like(state_ref)
        halo_ref[...] = jnp.zeros_like(halo_ref)

    cw = cw_ref[...]
    alog = alog_ref[...]
    dtb = dtb_ref[...]
    onorm = onorm_ref[...]
    ri = lax.broadcasted_iota(jnp.int32, (C, C), 0)
    ci = lax.broadcasted_iota(jnp.int32, (C, C), 1)
    lower_incl = ri >= ci
    strict = ri > ci
    tril_ones = lower_incl.astype(F32)

    chains = []
    for b in range(nseq):
        for c in range(nchunks):
            r0 = c * C
            prev = halo_ref[b] if c == 0 else qkv_ref[b, r0 - halo:r0, :]
            win = jnp.concatenate([prev, qkv_ref[b, r0:r0 + C, :]], axis=0)
            acc = cw[last:last + 1, :] * win[halo:, :]
            for j in range(last):
                acc = acc + cw[j:j + 1, :] * pltpu.roll(win, last - j, axis=0)[halo:, :]
            qkv = _silu(acc)

            ab = ab_ref[b, r0:r0 + C, :]
            xa = ab + dtb
            softplus = jnp.maximum(xa, 0.0) + jnp.log1p(jnp.exp(-jnp.abs(xa)))
            g = (-LOG2_E) * jnp.exp(alog) * softplus
            gc = _dot(tril_ones, g, HIGHEST)
            gc_t = gc.T
            beta_all = _sigmoid(ab)

            for h in heads:
                q = qkv[:, h * D:(h + 1) * D]
                k = qkv[:, W + h * D:W + (h + 1) * D]
                v = qkv[:, 2 * W + h * D:2 * W + (h + 1) * D]
                q = q * lax.rsqrt(jnp.sum(q * q, axis=-1, keepdims=True) + L2_EPS) * (D ** -0.5)
                k = k * lax.rsqrt(jnp.sum(k * k, axis=-1, keepdims=True) + L2_EPS)
                g_col = gc[:, h:h + 1]
                g_row = gc_t[h:h + 1, :]
                beta = beta_all[:, GDN_HEADS + h:GDN_HEADS + h + 1]
                decay = jnp.exp2(jnp.where(lower_incl, g_col - g_row, NEG_BIG))
                kb = k * beta
                g_last = g_col[C - 1:C, :]
                grown = jnp.exp2(g_col)
                chains.append(dict(
                    decay=decay,
                    kb=kb.astype(BF16),
                    k=k.astype(BF16),
                    q=q.astype(BF16),
                    rhs=jnp.concatenate([v * beta, kb * grown], axis=1).astype(BF16),
                    q_dec=(q * grown).astype(BF16),
                    k_dec_t=(k * jnp.exp2(g_last - g_col)).T.astype(BF16),
                    carry=jnp.exp2(g_last),
                ))

    m_list = [jnp.where(strict, _dot_nt(p["kb"], p["k"]) * p["decay"], 0.0) for p in chains]
    t_list = _unit_lower_inverses(m_list, ri, ci)
    sols = [_dot(t.astype(BF16), p["rhs"]) for t, p in zip(t_list, chains)]
    intras = [(_dot_nt(p["q"], p["k"]) * p["decay"]).astype(BF16) for p in chains]

    lanes = [(b, h) for b in range(nseq) for h in heads]
    states = [state_ref[b, h] for b, h in lanes]
    for c in range(nchunks):
        r0 = c * C
        idx = [(b * nchunks + c) * GDN_HEADS + h for b, h in lanes]
        sbs = [s.astype(BF16) for s in states]
        v_news = [(sols[i][:, :D] - _dot(sols[i][:, D:].astype(BF16), sb)).astype(BF16) for i, sb in zip(idx, sbs)]
        outs = [_dot(chains[i]["q_dec"], sb) + _dot(intras[i], vn) for i, sb, vn in zip(idx, sbs, v_news)]
        states = [s * chains[i]["carry"] + _dot(chains[i]["k_dec_t"], vn) for i, s, vn in zip(idx, states, v_news)]
        for (b, h), out in zip(lanes, outs):
            y = _rms(out) * onorm * _silu(gate_ref[b, r0:r0 + C, h * D:(h + 1) * D])
            o_ref[b, r0:r0 + C, h * D:(h + 1) * D] = y
    for (b, h), s in zip(lanes, states):
        state_ref[b, h] = s
    for b in range(nseq):
        halo_ref[b] = qkv_ref[b, rows_blk - halo:rows_blk, :]


def _gdn_mixer(qkv, gate, ab, conv_w, a_log, dt_bias, out_norm, layer):
    B, S, _ = qkv.shape
    sb = GDN_SEQ_BLOCK
    ns = GDN_SEQS
    assert B % ns == 0 and S % sb == 0
    blk = lambda b, s: (b, s, 0)
    return pl.pallas_call(
        _gdn_kernel,
        name="gated_deltanet",
        grid=(B // ns, S // sb),
        in_specs=[
            pl.BlockSpec((ns, sb, 3 * GDN_WIDTH), blk),
            pl.BlockSpec((ns, sb, GDN_WIDTH), blk),
            pl.BlockSpec((ns, sb, V7X_LANES), blk),
        ] + [_layer_spec(t, layer) for t in (conv_w, a_log, dt_bias, out_norm)],
        out_specs=pl.BlockSpec((ns, sb, GDN_WIDTH), blk),
        out_shape=jax.ShapeDtypeStruct((B, S, GDN_WIDTH), F32),
        scratch_shapes=[
            pltpu.VMEM((ns, GDN_HEADS, GDN_HEAD_DIM, GDN_HEAD_DIM), F32),
            pltpu.VMEM((ns, V7X_SUBLANES, 3 * GDN_WIDTH), F32),
        ],
        compiler_params=pltpu.CompilerParams(
            dimension_semantics=("arbitrary", "arbitrary"), vmem_limit_bytes=V7X_VMEM_LIMIT_BYTES),
    )(qkv, gate, ab, conv_w, a_log, dt_bias, out_norm)


def _out_ffn_kernel(x_ref, attn_ref, conv_ref, gdn_ref, wo_ref, g_post_ref, g_pre_ref, wg_ref, wu_ref, wd_ref,
                    g_ffn_ref, o_ref):
    a0, a1, a2 = ATTN_WIDTH, ATTN_WIDTH + CONV_CH, ATTN_WIDTH + CONV_CH + GDN_WIDTH
    tm = x_ref.shape[0]
    halves = [(0, tm // 2), (tm // 2, tm)]
    ys = []
    for lo, hi in halves:
        y = _dot(attn_ref[lo:hi, :].astype(BF16), wo_ref[0:a0, :])
        y = y + _dot(conv_ref[lo:hi, :].astype(BF16), wo_ref[a0:a1, :])
        y = y + _dot(gdn_ref[lo:hi, :].astype(BF16), wo_ref[a1:a2, :])
        ys.append(y)
    x1s = [x_ref[lo:hi, :] + _rms(y) * g_post_ref[...] for (lo, hi), y in zip(halves, ys)]
    hs = [(_rms(x1) * g_pre_ref[...]).astype(BF16) for x1 in x1s]
    acts = [(_silu(_dot(h, wg_ref[...])) * _dot(h, wu_ref[...])).astype(BF16) for h in hs]
    fs = [_dot(act, wd_ref[...]) for act in acts]
    for (lo, hi), x1, f in zip(halves, x1s, fs):
        o_ref[lo:hi, :] = x1 + _rms(f) * g_ffn_ref[...]


def _out_ffn(x2d, attn, conv, gdn, wo, g_post, g_pre, wg, wu, wd, g_ffn, layer, tm):
    T, D = x2d.shape
    row = lambda i: (i, 0)
    resident = lambda t: _layer_spec(t, layer, pipeline_mode=pl.Buffered(1))
    return pl.pallas_call(
        _out_ffn_kernel,
        name="outproj_ffn",
        grid=(T // tm,),
        in_specs=[
            pl.BlockSpec((tm, D), row),
            pl.BlockSpec((tm, ATTN_WIDTH), row),
            pl.BlockSpec((tm, CONV_CH), row),
            pl.BlockSpec((tm, GDN_WIDTH), row),
            resident(wo),
            _layer_spec(g_post, layer),
            _layer_spec(g_pre, layer),
            resident(wg),
            resident(wu),
            resident(wd),
            _layer_spec(g_ffn, layer),
        ],
        out_specs=pl.BlockSpec((tm, D), row),
        out_shape=jax.ShapeDtypeStruct((T, D), F32),
        compiler_params=pltpu.CompilerParams(
            dimension_semantics=("arbitrary",), vmem_limit_bytes=V7X_VMEM_LIMIT_BYTES),
    )(x2d, attn, conv, gdn, wo, g_post, g_pre, wg, wu, wd, g_ffn)


def _pack_in_weight(w_in):
    depth, d_model, in_width = w_in.shape
    wb = w_in.astype(BF16)
    half = ATTN_HEAD_DIM // 2
    qk = wb[:, :, :2 * ATTN_WIDTH].reshape(depth, d_model, 2, ATTN_HEADS, 2, half)
    qk = qk.transpose(0, 1, 2, 4, 3, 5).reshape(depth, d_model, 2 * ATTN_WIDTH)
    pad = jnp.zeros((depth, d_model, W_TOTAL - in_width), BF16)
    return jnp.concatenate([qk, wb[:, :, 2 * ATTN_WIDTH:], pad], axis=-1)


def _rotary_tables(seq):
    half = ATTN_HEAD_DIM // 2
    inv_freq = jnp.exp(-math.log(ROPE_THETA) * jnp.arange(half, dtype=F32) * (2.0 / ATTN_HEAD_DIM))
    ang = jnp.arange(seq).astype(F32)[:, None] * inv_freq[None, :]
    return jnp.tile(jnp.cos(ang), (1, ATTN_HEADS)), jnp.tile(jnp.sin(ang), (1, ATTN_HEADS))


def _rows(t, lanes=None):
    if lanes is not None:
        t = jnp.pad(t, ((0, 0), (0, lanes - t.shape[-1])))
    return t[:, None, :]


def kernel(x, attn_pre_norm, w_in, conv_dw, conv_dw_bias, conv_norm_gain, conv_norm_bias, gdn_short_conv, gdn_a_log,
           gdn_dt_bias, gdn_out_norm, w_out, attn_post_norm, ffn_pre_norm, w_gate, w_up, w_down, ffn_post_norm):
    B, S, D = x.shape
    depth = w_in.shape[0]
    T = B * S
    tm = ROW_TILE
    w_in_packed = _pack_in_weight(w_in)
    w_out_b, w_gate_b, w_up_b, w_down_b = (t.astype(BF16) for t in (w_out, w_gate, w_up, w_down))
    cos, sin = _rotary_tables(S)
    pre_gain, post_gain, ffn_pre_gain, ffn_post_gain = (
        _rows(t) for t in (attn_pre_norm, attn_post_norm, ffn_pre_norm, ffn_post_norm))
    conv_bias, conv_gain, conv_nbias = (_rows(t) for t in (conv_dw_bias, conv_norm_gain, conv_norm_bias))
    a_log, dt_bias, out_norm = _rows(gdn_a_log, V7X_LANES), _rows(gdn_dt_bias, V7X_LANES), _rows(gdn_out_norm)
    x2d = x.reshape(T, D)
    for i in range(depth):
        *attn_parts, conv_u, gdn_qkv, gdn_gate, gdn_ab = _inproj(x2d, pre_gain, cos, sin, w_in_packed, i, S, tm)
        y_attn = _attention([t.reshape(B, S, -1) for t in attn_parts])
        y_conv = _conv_mixer(conv_u.reshape(B, S, -1), conv_dw, conv_bias, conv_gain, conv_nbias, i)
        y_gdn = _gdn_mixer(gdn_qkv.reshape(B, S, -1), gdn_gate.reshape(B, S, -1), gdn_ab.reshape(B, S, -1),
                           gdn_short_conv, a_log, dt_bias, out_norm, i)
        x2d = _out_ffn(x2d, y_attn.reshape(T, -1), y_conv.reshape(T, -1), y_gdn.reshape(T, -1), w_out_b,
                       post_gain, ffn_pre_gain, w_gate_b, w_up_b, w_down_b, ffn_post_gain, i, tm)
    return x2d.reshape(B, S, D)
```

```python
import math

import jax
import jax.numpy as jnp
from jax import lax
from jax.experimental import pallas as pl
from jax.experimental.pallas import tpu as pltpu

F32 = jnp.float32
BF16 = jnp.bfloat16

ATTN_HEAD_DIM = 64
ATTN_HEADS = 4
ATTN_WIDTH = ATTN_HEADS * ATTN_HEAD_DIM
ATTN_BLOCK = 128
ATTN_GROUP = 8
DILATIONS = (1, 4, 16)
ROPE_THETA = 10000.0
CONV_CH = 256
CONV_WIDTH = 31
GDN_HEADS = 4
GDN_HEAD_DIM = 128
GDN_WIDTH = GDN_HEADS * GDN_HEAD_DIM
GDN_CONV_WIDTH = 4
GDN_CHUNK = 128
RMS_EPS = 1e-6
LN_EPS = 1e-5
L2_EPS = 1e-6
NEG_BIG = -1e30
LOG2_E = math.log2(math.e)

ROW_TILE = 512
FFN_ROW_TILE = 1024
FFN_PIECE_ROWS = 256

V7X_LANES = 128
V7X_SUBLANES = 8
V7X_VMEM_LIMIT_BYTES = 60 * 1024 * 1024

W_ATTN = 0
W_CONV = W_ATTN + 3 * ATTN_WIDTH
W_GQKV = W_CONV + 2 * CONV_CH
W_GGATE = W_GQKV + 3 * GDN_WIDTH
W_AB = W_GGATE + GDN_WIDTH
W_TOTAL = W_AB + V7X_LANES


def _rms(x):
    return x * lax.rsqrt(jnp.mean(x * x, axis=-1, keepdims=True) + RMS_EPS)


def _sigmoid(x):
    return 1.0 / (1.0 + jnp.exp(-x))


def _silu(x):
    return x * _sigmoid(x)


def _dot(a, b, precision=None):
    return jnp.dot(a, b, preferred_element_type=F32, precision=precision)


def _dot_nt(a, b, precision=None):
    return lax.dot_general(a, b, (((1,), (1,)), ((), ())), preferred_element_type=F32, precision=precision)


def _inproj_kernel(x_ref, gain_ref, cos_ref, sin_ref, w_ref, q1_ref, q2_ref, k1_ref, k2_ref, va_ref, vb_ref,
                   conv_ref, gqkv_ref, ggate_ref, ab_ref):
    x = x_ref[...]
    h = (x * gain_ref[...]).astype(BF16)
    inv_rms = lax.rsqrt(jnp.mean(x * x, axis=-1, keepdims=True) + RMS_EPS)

    def proj(lo, hi):
        return _dot(h, w_ref[:, lo:hi]) * inv_rms

    qk = proj(W_ATTN, W_ATTN + 2 * ATTN_WIDTH)
    cos = cos_ref[...]
    sin = sin_ref[...]
    half = ATTN_WIDTH // 2
    q1, q2 = qk[:, 0:half], qk[:, half:2 * half]
    k1, k2 = qk[:, 2 * half:3 * half], qk[:, 3 * half:4 * half]
    scale = ATTN_HEAD_DIM ** -0.5 * LOG2_E
    q1_ref[...] = (q1 * cos - q2 * sin) * scale
    q2_ref[...] = (q2 * cos + q1 * sin) * scale
    k1_ref[...] = k1 * cos - k2 * sin
    k2_ref[...] = k2 * cos + k1 * sin
    v = proj(W_ATTN + 2 * ATTN_WIDTH, W_CONV)
    va_ref[...] = v[:, 0:half]
    vb_ref[...] = v[:, half:2 * half]
    conv_ref[...] = proj(W_CONV, W_GQKV)
    gqkv_ref[...] = proj(W_GQKV, W_GGATE)
    ggate_ref[...] = proj(W_GGATE, W_AB)
    ab_ref[...] = proj(W_AB, W_TOTAL)


def _layer_spec(stacked, layer, **kwargs):
    _, rows, cols = stacked.shape
    return pl.BlockSpec((None, rows, cols), lambda *_: (layer, 0, 0), **kwargs)


def _inproj(x2d, gain, cos, sin, w, layer, seq, tm):
    T, D = x2d.shape
    blocks_per_seq = seq // tm
    row = lambda i: (i, 0)
    widths = (ATTN_WIDTH // 2,) * 6 + (2 * CONV_CH, 3 * GDN_WIDTH, GDN_WIDTH, V7X_LANES)
    return pl.pallas_call(
        _inproj_kernel,
        name="inproj",
        grid=(T // tm,),
        in_specs=[
            pl.BlockSpec((tm, D), row),
            _layer_spec(gain, layer),
            pl.BlockSpec((tm, ATTN_WIDTH // 2), lambda i: (i % blocks_per_seq, 0)),
            pl.BlockSpec((tm, ATTN_WIDTH // 2), lambda i: (i % blocks_per_seq, 0)),
            _layer_spec(w, layer),
        ],
        out_specs=[pl.BlockSpec((tm, n), row) for n in widths],
        out_shape=[jax.ShapeDtypeStruct((T, n), F32) for n in widths],
        compiler_params=pltpu.CompilerParams(
            dimension_semantics=("arbitrary",), vmem_limit_bytes=V7X_VMEM_LIMIT_BYTES),
    )(x2d, gain, cos, sin, w)


def _attn_units(units):
    heads = range(ATTN_HEADS)
    lane = lax.broadcasted_iota(jnp.int32, (1, ATTN_WIDTH), 1)
    qk_head = (lane % (ATTN_WIDTH // 2)) // (ATTN_HEAD_DIM // 2)
    scores = []
    for q, k, _, _ in units:
        qb = q.astype(BF16)
        zero = jnp.zeros_like(qb)
        q_stack = jnp.concatenate([jnp.where(qk_head == h, qb, zero) for h in heads], axis=0)
        scores.append(_dot_nt(k.astype(BF16), q_stack))
    stats = []
    for s, (_, _, _, valid_t) in zip(scores, units):
        s = jnp.where(jnp.concatenate([valid_t] * ATTN_HEADS, axis=1), s, NEG_BIG)
        m = jnp.max(s, axis=0, keepdims=True)
        p = jnp.exp2(s - m)
        l = jnp.sum(p, axis=0, keepdims=True)
        stats.append((p.astype(BF16), 1.0 / l, m + jnp.log2(l)))
    outs = [lax.dot_general(v.astype(BF16), p, (((0,), (0,)), ((), ())), preferred_element_type=F32)
            for (p, _, _), (_, _, v, _) in zip(stats, units)]
    results = []
    for o_t, (_, inv_l, lse), (q, _, _, _) in zip(outs, stats, units):
        nq = q.shape[0]
        d = ATTN_HEAD_DIM
        out_t = jnp.concatenate(
            [o_t[h * d:(h + 1) * d, h * nq:(h + 1) * nq] * inv_l[:, h * nq:(h + 1) * nq] for h in heads], axis=0)
        lse_t = jnp.concatenate(
            [jnp.broadcast_to(lse[:, h * nq:(h + 1) * nq], (d, nq)) for h in heads], axis=0)
        results.append((out_t.T, lse_t.T))
    return results


def _attn_kernel(q1_ref, q2_ref, k1_ref, k2_ref, va_ref, vb_ref, o_ref,
                 o2a_ref, o2b_ref, l2a_ref, l2b_ref, o3a_ref, o3b_ref, l3a_ref, l3b_ref):
    n = ATTN_BLOCK
    seq = q1_ref.shape[0]
    ki = lax.broadcasted_iota(jnp.int32, (2 * n, n), 0)
    qi = lax.broadcasted_iota(jnp.int32, (2 * n, n), 1)
    valid_band = (ki >= qi) & (ki <= qi + n)
    valid_first = lax.broadcasted_iota(jnp.int32, (n, n), 0) <= lax.broadcasted_iota(jnp.int32, (n, n), 1)

    def rows(start, size, stride):
        if stride == 1:
            return pl.ds(start, size)
        return pl.ds(start, size, stride=stride)

    def load(refs, sl):
        return jnp.concatenate([r[sl, :] for r in refs], axis=1)

    def store(refs, sl, val):
        for j, r in enumerate(refs):
            r[sl, :] = val[:, j * V7X_LANES:(j + 1) * V7X_LANES]

    def run_units(specs):
        units = []
        for q_start, stride, first in specs:
            q = load((q1_ref, q2_ref), rows(q_start, n, stride))
            if first:
                k_rows = rows(q_start, n, stride)
                valid = valid_first
            else:
                k_rows = rows(q_start - n * stride, 2 * n, stride)
                valid = valid_band
            units.append((q, load((k1_ref, k2_ref), k_rows), load((va_ref, vb_ref), k_rows), valid))
        return _attn_units(units)

    def run_dilated(specs, o_refs, l_refs):
        for (q_start, stride, _), (o, lse) in zip(specs, run_units(specs)):
            store(o_refs, rows(q_start, n, stride), o)
            store(l_refs, rows(q_start, n, stride), lse)

    d2, d3 = DILATIONS[1], DILATIONS[2]
    blocks2 = seq // (n * d2)
    residues2 = ATTN_GROUP // blocks2
    assert seq == n * d3 and d3 % ATTN_GROUP == 0 and ATTN_GROUP % blocks2 == 0 and d2 % residues2 == 0
    assert (seq // n) % ATTN_GROUP == 0

    def widest_body(g, carry):
        specs = [(g * ATTN_GROUP + u, d3, True) for u in range(ATTN_GROUP)]
        run_dilated(specs, (o3a_ref, o3b_ref), (l3a_ref, l3b_ref))
        return carry

    lax.fori_loop(0, d3 // ATTN_GROUP, widest_body, 0)

    def middle_body(g, carry):
        specs = [(b * n * d2 + g * residues2 + j, d2, b == 0) for j in range(residues2) for b in range(blocks2)]
        run_dilated(specs, (o2a_ref, o2b_ref), (l2a_ref, l2b_ref))
        return carry

    lax.fori_loop(0, d2 // residues2, middle_body, 0)

    def run_dense(specs):
        for (start, _, _), (o1, l1) in zip(specs, run_units(specs)):
            sl = pl.ds(start, n)
            o2, l2 = load((o2a_ref, o2b_ref), sl), load((l2a_ref, l2b_ref), sl)
            o3, l3 = load((o3a_ref, o3b_ref), sl), load((l3a_ref, l3b_ref), sl)
            top = jnp.maximum(jnp.maximum(l1, l2), l3)
            w1, w2, w3 = jnp.exp2(l1 - top), jnp.exp2(l2 - top), jnp.exp2(l3 - top)
            o_ref[sl, :] = (w1 * o1 + w2 * o2 + w3 * o3) / (w1 + w2 + w3)

    run_dense([(u * n, 1, u == 0) for u in range(ATTN_GROUP)])

    def dense_body(g, carry):
        base = pl.multiple_of(g * (ATTN_GROUP * n), ATTN_GROUP * n)
        run_dense([(base + u * n, 1, False) for u in range(ATTN_GROUP)])
        return carry

    lax.fori_loop(1, seq // (ATTN_GROUP * n), dense_body, 0)


def _attention(parts):
    B, S, W = parts[0].shape
    blk = lambda b: (b, 0, 0)
    return pl.pallas_call(
        _attn_kernel,
        name="dilated_attn",
        grid=(B,),
        in_specs=[pl.BlockSpec((None, S, W), blk) for _ in parts],
        out_specs=pl.BlockSpec((None, S, ATTN_WIDTH), blk),
        out_shape=jax.ShapeDtypeStruct((B, S, ATTN_WIDTH), F32),
        scratch_shapes=[pltpu.VMEM((S, V7X_LANES), F32) for _ in range(8)],
        compiler_params=pltpu.CompilerParams(
            dimension_semantics=("arbitrary",), vmem_limit_bytes=V7X_VMEM_LIMIT_BYTES),
    )(*parts)


CONV_PAD = 32
CONV_ROWS = 2048


def _conv_kernel(u_ref, dw_ref, bias_ref, gain_ref, nbias_ref, o_ref, g_ref):
    seq = u_ref.shape[0]
    g_ref[0:CONV_PAD, :] = jnp.zeros((CONV_PAD, CONV_CH), F32)
    g_ref[CONV_PAD:CONV_PAD + seq, :] = u_ref[:, 0:CONV_CH] * _sigmoid(u_ref[:, CONV_CH:2 * CONV_CH])
    dw = dw_ref[...]
    bias = bias_ref[...]
    gain = gain_ref[...]
    nbias = nbias_ref[...]
    lead = CONV_PAD - (CONV_WIDTH - 1)
    sub = V7X_SUBLANES

    def body(c, carry):
        r0 = pl.multiple_of(c * CONV_ROWS, CONV_ROWS)
        acc = jnp.zeros((CONV_ROWS, CONV_CH), F32) + bias
        for rem in range(sub):
            taps = [j for j in range(CONV_WIDTH) if (lead + j) % sub == rem]
            nrows = CONV_ROWS if rem == 0 else CONV_ROWS + sub
            part = None
            for j in taps:
                start = pl.multiple_of(r0 + (lead + j - rem), sub)
                term = dw[j:j + 1, :] * g_ref[pl.ds(start, nrows), :]
                part = term if part is None else part + term
            if rem:
                part = pltpu.roll(part, nrows - rem, axis=0)[0:CONV_ROWS, :]
            acc = acc + part
        mu = jnp.mean(acc, axis=-1, keepdims=True)
        cen = acc - mu
        var = jnp.mean(cen * cen, axis=-1, keepdims=True)
        y = cen * lax.rsqrt(var + LN_EPS) * gain + nbias
        o_ref[pl.ds(r0, CONV_ROWS), :] = _silu(y)
        return carry

    lax.fori_loop(0, seq // CONV_ROWS, body, 0)


def _conv_mixer(u, dw, bias, gain, nbias, layer):
    B, S, W = u.shape
    blk = lambda b: (b, 0, 0)
    return pl.pallas_call(
        _conv_kernel,
        name="conformer_conv",
        grid=(B,),
        in_specs=[pl.BlockSpec((None, S, W), blk)] + [_layer_spec(t, layer) for t in (dw, bias, gain, nbias)],
        out_specs=pl.BlockSpec((None, S, CONV_CH), blk),
        out_shape=jax.ShapeDtypeStruct((B, S, CONV_CH), F32),
        scratch_shapes=[pltpu.VMEM((CONV_PAD + S, CONV_CH), F32)],
        compiler_params=pltpu.CompilerParams(
            dimension_semantics=("arbitrary",), vmem_limit_bytes=V7X_VMEM_LIMIT_BYTES),
    )(u, dw, bias, gain, nbias)


GDN_SEQ_BLOCK = 256
GDN_SEQS = 2
HIGHEST = lax.Precision.HIGHEST


def _unit_lower_inverses(m_list, rows, cols):
    size = m_list[0].shape[0]
    eye = (rows == cols).astype(F32)
    link = (rows // 2 == cols // 2) & (rows == cols + 1)
    ts = [eye - jnp.where(link, m, 0.0) for m in m_list]
    neg_mbs = [(-m).astype(BF16) for m in m_list]
    step = 2
    while step < size:
        link = ((rows // step) == (cols // step) + 1) & ((rows // (2 * step)) == (cols // (2 * step)))
        tbs = [t.astype(BF16) for t in ts]
        mts = [_dot(mb, tb).astype(BF16) for mb, tb in zip(neg_mbs, tbs)]
        ts = [jnp.where(link, _dot(tb, mt), t) for t, tb, mt in zip(ts, tbs, mts)]
        step *= 2
    return ts


def _gdn_kernel(qkv_ref, gate_ref, ab_ref, cw_ref, alog_ref, dtb_ref, onorm_ref, o_ref, state_ref, halo_ref):
    C = GDN_CHUNK
    D = GDN_HEAD_DIM
    W = GDN_WIDTH
    nseq, rows_blk = qkv_ref.shape[0], qkv_ref.shape[1]
    nchunks = rows_blk // C
    halo = halo_ref.shape[1]
    last = GDN_CONV_WIDTH - 1
    heads = range(GDN_HEADS)

    @pl.when(pl.program_id(1) == 0)
    def _():
        state_ref[...] = jnp.zeros_like(state_ref)
        halo_ref[...] = jnp.zeros_like(halo_ref)

    cw = cw_ref[...]
    alog = alog_ref[...]
    dtb = dtb_ref[...]
    onorm = onorm_ref[...]
    ri = lax.broadcasted_iota(jnp.int32, (C, C), 0)
    ci = lax.broadcasted_iota(jnp.int32, (C, C), 1)
    lower_incl = ri >= ci
    strict = ri > ci
    tril_ones = lower_incl.astype(F32)

    chains = []
    for b in range(nseq):
        for c in range(nchunks):
            r0 = c * C
            prev = halo_ref[b] if c == 0 else qkv_ref[b, r0 - halo:r0, :]
            win = jnp.concatenate([prev, qkv_ref[b, r0:r0 + C, :]], axis=0)
            acc = cw[last:last + 1, :] * win[halo:, :]
            for j in range(last):
                acc = acc + cw[j:j + 1, :] * pltpu.roll(win, last - j, axis=0)[halo:, :]
            qkv = _silu(acc)

            ab = ab_ref[b, r0:r0 + C, :]
            xa = ab + dtb
            softplus = jnp.maximum(xa, 0.0) + jnp.log1p(jnp.exp(-jnp.abs(xa)))
            g = (-LOG2_E) * jnp.exp(alog) * softplus
            gc = _dot(tril_ones, g, HIGHEST)
            gc_t = gc.T
            beta_all = _sigmoid(ab)

            for h in heads:
                q = qkv[:, h * D:(h + 1) * D]
                k = qkv[:, W + h * D:W + (h + 1) * D]
                v = qkv[:, 2 * W + h * D:2 * W + (h + 1) * D]
                q = q * lax.rsqrt(jnp.sum(q * q, axis=-1, keepdims=True) + L2_EPS) * (D ** -0.5)
                k = k * lax.rsqrt(jnp.sum(k * k, axis=-1, keepdims=True) + L2_EPS)
                g_col = gc[:, h:h + 1]
                g_row = gc_t[h:h + 1, :]
                beta = beta_all[:, GDN_HEADS + h:GDN_HEADS + h + 1]
                decay = jnp.exp2(jnp.where(lower_incl, g_col - g_row, NEG_BIG))
                kb = k * beta
                g_last = g_col[C - 1:C, :]
                grown = jnp.exp2(g_col)
                chains.append(dict(
                    decay=decay,
                    kb=kb.astype(BF16),
                    k=k.astype(BF16),
                    q=q.astype(BF16),
                    rhs=jnp.concatenate([v * beta, kb * grown], axis=1).astype(BF16),
                    q_dec=(q * grown).astype(BF16),
                    k_dec_t=(k * jnp.exp2(g_last - g_col)).T.astype(BF16),
                    carry=jnp.exp2(g_last),
                ))

    m_list = [jnp.where(strict, _dot_nt(p["kb"], p["k"]) * p["decay"], 0.0) for p in chains]
    t_list = _unit_lower_inverses(m_list, ri, ci)
    sols = [_dot(t.astype(BF16), p["rhs"]) for t, p in zip(t_list, chains)]
    intras = [(_dot_nt(p["q"], p["k"]) * p["decay"]).astype(BF16) for p in chains]

    lanes = [(b, h) for b in range(nseq) for h in heads]
    states = [state_ref[b, h] for b, h in lanes]
    for c in range(nchunks):
        r0 = c * C
        idx = [(b * nchunks + c) * GDN_HEADS + h for b, h in lanes]
        sbs = [s.astype(BF16) for s in states]
        v_news = [(sols[i][:, :D] - _dot(sols[i][:, D:].astype(BF16), sb)).astype(BF16) for i, sb in zip(idx, sbs)]
        outs = [_dot(chains[i]["q_dec"], sb) + _dot(intras[i], vn) for i, sb, vn in zip(idx, sbs, v_news)]
        states = [s * chains[i]["carry"] + _dot(chains[i]["k_dec_t"], vn) for i, s, vn in zip(idx, states, v_news)]
        for (b, h), out in zip(lanes, outs):
            y = _rms(out) * onorm * _silu(gate_ref[b, r0:r0 + C, h * D:(h + 1) * D])
            o_ref[b, r0:r0 + C, h * D:(h + 1) * D] = y
    for (b, h), s in zip(lanes, states):
        state_ref[b, h] = s
    for b in range(nseq):
        halo_ref[b] = qkv_ref[b, rows_blk - halo:rows_blk, :]


def _gdn_mixer(qkv, gate, ab, conv_w, a_log, dt_bias, out_norm, layer):
    B, S, _ = qkv.shape
    sb = GDN_SEQ_BLOCK
    ns = GDN_SEQS
    assert B % ns == 0 and S % sb == 0
    blk = lambda b, s: (b, s, 0)
    return pl.pallas_call(
        _gdn_kernel,
        name="gated_deltanet",
        grid=(B // ns, S // sb),
        in_specs=[
            pl.BlockSpec((ns, sb, 3 * GDN_WIDTH), blk),
            pl.BlockSpec((ns, sb, GDN_WIDTH), blk),
            pl.BlockSpec((ns, sb, V7X_LANES), blk),
        ] + [_layer_spec(t, layer) for t in (conv_w, a_log, dt_bias, out_norm)],
        out_specs=pl.BlockSpec((ns, sb, GDN_WIDTH), blk),
        out_shape=jax.ShapeDtypeStruct((B, S, GDN_WIDTH), F32),
        scratch_shapes=[
            pltpu.VMEM((ns, GDN_HEADS, GDN_HEAD_DIM, GDN_HEAD_DIM), F32),
            pltpu.VMEM((ns, V7X_SUBLANES, 3 * GDN_WIDTH), F32),
        ],
        compiler_params=pltpu.CompilerParams(
            dimension_semantics=("arbitrary", "arbitrary"), vmem_limit_bytes=V7X_VMEM_LIMIT_BYTES),
    )(qkv, gate, ab, conv_w, a_log, dt_bias, out_norm)


def _out_ffn_kernel(x_ref, attn_ref, conv_ref, gdn_ref, wo_ref, g_post_ref, g_pre_ref, wg_ref, wu_ref, wd_ref,
                    g_ffn_ref, o_ref):
    a0, a1, a2 = ATTN_WIDTH, ATTN_WIDTH + CONV_CH, ATTN_WIDTH + CONV_CH + GDN_WIDTH
    tm = x_ref.shape[0]
    halves = [(lo, lo + FFN_PIECE_ROWS) for lo in range(0, tm, FFN_PIECE_ROWS)]
    ys = []
    for lo, hi in halves:
        y = _dot(attn_ref[lo:hi, :].astype(BF16), wo_ref[0:a0, :])
        y = y + _dot(conv_ref[lo:hi, :].astype(BF16), wo_ref[a0:a1, :])
        y = y + _dot(gdn_ref[lo:hi, :].astype(BF16), wo_ref[a1:a2, :])
        ys.append(y)
    x1s = [x_ref[lo:hi, :] + _rms(y) * g_post_ref[...] for (lo, hi), y in zip(halves, ys)]
    hs = [(_rms(x1) * g_pre_ref[...]).astype(BF16) for x1 in x1s]
    acts = [(_silu(_dot(h, wg_ref[...])) * _dot(h, wu_ref[...])).astype(BF16) for h in hs]
    fs = [_dot(act, wd_ref[...]) for act in acts]
    for (lo, hi), x1, f in zip(halves, x1s, fs):
        o_ref[lo:hi, :] = x1 + _rms(f) * g_ffn_ref[...]


def _out_ffn(x2d, attn, conv, gdn, wo, g_post, g_pre, wg, wu, wd, g_ffn, layer, tm):
    T, D = x2d.shape
    row = lambda i: (i, 0)
    resident = lambda t: _layer_spec(t, layer, pipeline_mode=pl.Buffered(1))
    return pl.pallas_call(
        _out_ffn_kernel,
        name="outproj_ffn",
        grid=(T // tm,),
        in_specs=[
            pl.BlockSpec((tm, D), row),
            pl.BlockSpec((tm, ATTN_WIDTH), row),
            pl.BlockSpec((tm, CONV_CH), row),
            pl.BlockSpec((tm, GDN_WIDTH), row),
            resident(wo),
            _layer_spec(g_post, layer),
            _layer_spec(g_pre, layer),
            resident(wg),
            resident(wu),
            resident(wd),
            _layer_spec(g_ffn, layer),
        ],
        out_specs=pl.BlockSpec((tm, D), row),
        out_shape=jax.ShapeDtypeStruct((T, D), F32),
        compiler_params=pltpu.CompilerParams(
            dimension_semantics=("arbitrary",), vmem_limit_bytes=V7X_VMEM_LIMIT_BYTES),
    )(x2d, attn, conv, gdn, wo, g_post, g_pre, wg, wu, wd, g_ffn)


def _pack_in_weight(w_in):
    depth, d_model, in_width = w_in.shape
    wb = w_in.astype(BF16)
    half = ATTN_HEAD_DIM // 2
    qk = wb[:, :, :2 * ATTN_WIDTH].reshape(depth, d_model, 2, ATTN_HEADS, 2, half)
    qk = qk.transpose(0, 1, 2, 4, 3, 5).reshape(depth, d_model, 2 * ATTN_WIDTH)
    pad = jnp.zeros((depth, d_model, W_TOTAL - in_width), BF16)
    return jnp.concatenate([qk, wb[:, :, 2 * ATTN_WIDTH:], pad], axis=-1)


def _rotary_tables(seq):
    half = ATTN_HEAD_DIM // 2
    inv_freq = jnp.exp(-math.log(ROPE_THETA) * jnp.arange(half, dtype=F32) * (2.0 / ATTN_HEAD_DIM))
    ang = jnp.arange(seq).astype(F32)[:, None] * inv_freq[None, :]
    return jnp.tile(jnp.cos(ang), (1, ATTN_HEADS)), jnp.tile(jnp.sin(ang), (1, ATTN_HEADS))


def _rows(t, lanes=None):
    if lanes is not None:
        t = jnp.pad(t, ((0, 0), (0, lanes - t.shape[-1])))
    return t[:, None, :]


def kernel(x, attn_pre_norm, w_in, conv_dw, conv_dw_bias, conv_norm_gain, conv_norm_bias, gdn_short_conv, gdn_a_log,
           gdn_dt_bias, gdn_out_norm, w_out, attn_post_norm, ffn_pre_norm, w_gate, w_up, w_down, ffn_post_norm):
    B, S, D = x.shape
    depth = w_in.shape[0]
    T = B * S
    tm = ROW_TILE
    w_in_packed = _pack_in_weight(w_in)
    w_out_b, w_gate_b, w_up_b, w_down_b = (t.astype(BF16) for t in (w_out, w_gate, w_up, w_down))
    cos, sin = _rotary_tables(S)
    pre_gain, post_gain, ffn_pre_gain, ffn_post_gain = (
        _rows(t) for t in (attn_pre_norm, attn_post_norm, ffn_pre_norm, ffn_post_norm))
    conv_bias, conv_gain, conv_nbias = (_rows(t) for t in (conv_dw_bias, conv_norm_gain, conv_norm_bias))
    a_log, dt_bias, out_norm = _rows(gdn_a_log, V7X_LANES), _rows(gdn_dt_bias, V7X_LANES), _rows(gdn_out_norm)
    x2d = x.reshape(T, D)
    for i in range(depth):
        *attn_parts, conv_u, gdn_qkv, gdn_gate, gdn_ab = _inproj(x2d, pre_gain, cos, sin, w_in_packed, i, S, tm)
        y_attn = _attention([t.reshape(B, S, -1) for t in attn_parts])
        y_conv = _conv_mixer(conv_u.reshape(B, S, -1), conv_dw, conv_bias, conv_gain, conv_nbias, i)
        y_gdn = _gdn_mixer(gdn_qkv.reshape(B, S, -1), gdn_gate.reshape(B, S, -1), gdn_ab.reshape(B, S, -1),
                           gdn_short_conv, a_log, dt_bias, out_norm, i)
        x2d = _out_ffn(x2d, y_attn.reshape(T, -1), y_conv.reshape(T, -1), y_gdn.reshape(T, -1), w_out_b,
                       post_gain, ffn_pre_gain, w_gate_b, w_up_b, w_down_b, ffn_post_gain, i, FFN_ROW_TILE)
    return x2d.reshape(B, S, D)
```
